```python
import jax, jax.numpy as jnp
from jax import lax
import numpy as np

D_MODEL = 1024
BATCH = 8
SEQ = 2048
DEPTH = 2

GRID_W = 64
CTX_LEN = 256
BRANCH_WIDTH = 512
N_BRANCH = 3
LRU_WIDTH = BRANCH_WIDTH
LRU_BLOCKS = 8
LRU_BLOCK = LRU_WIDTH // LRU_BLOCKS
LRU_C = 8.0
CONV_W = 4
CONV_LEFT = 2
HG_HEADS = 4
HG_DK = 128
HG_DV = BRANCH_WIDTH // HG_HEADS
HG_QK = HG_HEADS * HG_DK
HG_WIDTH = HG_HEADS * HG_DV
HG_CHUNK = 64
NA_HEADS = 8
NA_DH = BRANCH_WIDTH // NA_HEADS
NA_WIDTH = NA_HEADS * NA_DH
WIN_R = 8
WIN_C = 16
ROPE_BASE = 10000.0
MASK_VALUE = -1e30
D_FF = 4 * D_MODEL
EPS = 1e-6
PROJ_SIZES = (LRU_WIDTH, LRU_WIDTH, HG_QK, 2 * HG_QK, HG_WIDTH, HG_WIDTH, 3 * NA_WIDTH, N_BRANCH * D_MODEL)
D_IN = sum(PROJ_SIZES)

kernel_name = 'hybrid_rglru_hgrn2_natten_dit_block'

F32 = jnp.float32


def rms_norm(x, g):
    x32 = x.astype(F32)
    y = x32 * lax.rsqrt(jnp.mean(jnp.square(x32), axis=-1, keepdims=True) + EPS)
    return (y * g.astype(F32)).astype(x.dtype)


def modulation(cond, w, b):
    m = (jax.nn.silu(cond) @ w + b)[..., None, :]
    return jnp.split(m, 6, axis=-1)


def split_projection(p):
    idx = np.cumsum(PROJ_SIZES)[:-1].tolist()
    return jnp.split(p, idx, axis=-1)


def centred_depthwise_conv(x, w, b):
    T = x.shape[1]
    xp = jnp.pad(x, ((0, 0), (CONV_LEFT, CONV_W - 1 - CONV_LEFT), (0, 0)))
    out = b
    for j in range(CONV_W):
        out = out + xp[:, j:j + T] * w[j]
    return out


def _lin_combine(left, right):
    a_l, b_l = left
    a_r, b_r = right
    return a_l * a_r, a_r * b_l + b_r


def linear_scan(a, b, h0, reverse):
    a_cum, h = lax.associative_scan(_lin_combine, (a, b), reverse=reverse, axis=1)
    if h0 is not None:
        h = h + a_cum * h0[:, None]
    return h


def rglru_branch(x_in, gate_in, conv_w, conv_b, wa, ba, wx, bx, lam, h0):
    B, T, _ = x_in.shape
    xc = centred_depthwise_conv(x_in, conv_w, conv_b).astype(F32)
    xb = xc.reshape(B, T, LRU_BLOCKS, LRU_BLOCK)
    r = jax.nn.sigmoid(jnp.einsum('btni,dnio->dbtno', xb, wa.astype(F32)).reshape(2, B, T, LRU_WIDTH) + ba.astype(F32)[:, None, None])
    i = jax.nn.sigmoid(jnp.einsum('btni,dnio->dbtno', xb, wx.astype(F32)).reshape(2, B, T, LRU_WIDTH) + bx.astype(F32)[:, None, None])
    log_a = -LRU_C * r * jax.nn.softplus(-lam.astype(F32))[:, None, None]
    a = jnp.exp(log_a)
    b = jnp.sqrt(-jnp.expm1(2.0 * log_a)) * (i * xc[None])
    h0f, h0b = (None, None) if h0 is None else (h0[0], h0[1])
    hf = linear_scan(a[0], b[0], h0f, False)
    hb = linear_scan(a[1], b[1], h0b, True)
    y = ((hf + hb) * jax.nn.gelu(gate_in.astype(F32))).astype(x_in.dtype)
    return y, jnp.stack([hf[:, -1], hb[:, 0]])


def gated_linear_chunks(q, k, v, g, s0):
    B, T, H, _ = q.shape
    n = T // HG_CHUNK

    def chunks(t):
        return t.reshape(B, n, HG_CHUNK, H, t.shape[-1]).transpose(1, 0, 3, 2, 4)

    qc, kc, vc = chunks(q), chunks(k), chunks(v)
    bc = jnp.cumsum(chunks(g), axis=3)
    incl = jnp.tril(jnp.ones((HG_CHUNK, HG_CHUNK), dtype=bool))

    def step(S, inp):
        qi, ki, vi, bi = inp
        diff = bi[:, :, :, None, :] - bi[:, :, None, :, :]
        decay = jnp.where(incl[:, :, None], jnp.exp(jnp.minimum(diff, 0.0)), 0.0)
        att = jnp.einsum('bhtd,bhsd,bhtsd->bhts', qi, ki, decay)
        o = att @ vi + jnp.einsum('bhtd,bhde->bhte', qi * jnp.exp(bi), S)
        b_last = bi[:, :, -1:, :]
        S = jnp.exp(b_last[:, :, 0, :, None]) * S + jnp.einsum('bhsd,bhse->bhde', ki * jnp.exp(b_last - bi), vi)
        return S, o

    s_fin, o = lax.scan(step, s0, (qc, kc, vc, bc))
    return o.transpose(1, 0, 3, 2, 4).reshape(B, T, H, -1), s_fin


def hgrn2_direction(q, f_raw, v, lb, s0, reverse):
    f = lb + (1.0 - lb) * jax.nn.sigmoid(f_raw)
    g = jnp.log(f)
    k = 1.0 - f
    if reverse:
        q, k, v, g = (jnp.flip(t, axis=1) for t in (q, k, v, g))
    o, s = gated_linear_chunks(q, k, v, g, s0)
    if reverse:
        o = jnp.flip(o, axis=1)
    return o, s


def hgrn2_branch(q_raw, f_raw, i_raw, og_raw, lb, norm_g, s0):
    B, T, _ = q_raw.shape
    q = jax.nn.silu(q_raw.astype(F32)).reshape(B, T, HG_HEADS, HG_DK)
    f = f_raw.astype(F32).reshape(B, T, 2, HG_HEADS, HG_DK)
    v = i_raw.astype(F32).reshape(B, T, HG_HEADS, HG_DV)
    lbh = lb.reshape(2, HG_HEADS, HG_DK)
    if s0 is None:
        s0 = jnp.zeros((2, B, HG_HEADS, HG_DK, HG_DV), F32)
    o_f, s_f = hgrn2_direction(q, f[:, :, 0], v, lbh[0], s0[0], False)
    o_b, s_b = hgrn2_direction(q, f[:, :, 1], v, lbh[1], s0[1], True)
    o = rms_norm(o_f + o_b, norm_g) * jax.nn.sigmoid(og_raw.astype(F32).reshape(B, T, HG_HEADS, HG_DV))
    return o.reshape(B, T, HG_WIDTH).astype(q_raw.dtype), jnp.stack([s_f, s_b])


def na_heads(qkv, q_g, k_g):
    B, T, _ = qkv.shape
    qkv = qkv.reshape(B, T, 3, NA_HEADS, NA_DH)
    return rms_norm(qkv[:, :, 0], q_g), rms_norm(qkv[:, :, 1], k_g), qkv[:, :, 2]


def axial_rope(x):
    T, dh = x.shape[1], x.shape[-1]
    half = dh // 2
    nf = half // 2
    t = jnp.arange(T)
    inv_freq = ROPE_BASE ** (-jnp.arange(nf, dtype=F32) / nf)

    def rotate(xh, pos):
        ang = pos.astype(F32)[:, None] * inv_freq
        cos = jnp.cos(ang)[None, :, None, :]
        sin = jnp.sin(ang)[None, :, None, :]
        x1 = xh[..., :nf].astype(F32)
        x2 = xh[..., nf:].astype(F32)
        return jnp.concatenate([x1 * cos - x2 * sin, x2 * cos + x1 * sin], axis=-1)

    out = jnp.concatenate([rotate(x[..., :half], t // GRID_W), rotate(x[..., half:], t % GRID_W)], axis=-1)
    return out.astype(x.dtype)


def neighbourhood_attention(q, k, v, q_plain, k_ctx, v_ctx, rpb):
    B, T, H, dh = q.shape
    rows = T // GRID_W
    wr = min(WIN_R, rows)
    scale = dh ** -0.5
    r = jnp.arange(rows)
    key_rows = jnp.clip(r - wr // 2, 0, rows - wr)[:, None] + jnp.arange(wr)
    col = jnp.arange(GRID_W)
    c_start = jnp.clip(col - WIN_C // 2, 0, GRID_W - WIN_C)
    col_ok = (col[None, :] >= c_start[:, None]) & (col[None, :] < c_start[:, None] + WIN_C)
    d_row = key_rows - r[:, None] + (WIN_R - 1)
    d_col = jnp.clip(col[None, :] - col[:, None], -(WIN_C - 1), WIN_C - 1) + (WIN_C - 1)
    bias = rpb.astype(F32)[:, d_row[:, None, :, None], d_col[None, :, None, :]]

    qg = q.reshape(B, rows, GRID_W, H, dh)
    kg = k.reshape(B, rows, GRID_W, H, dh)[:, key_rows]
    vg = v.reshape(B, rows, GRID_W, H, dh)[:, key_rows]
    s_loc = jnp.einsum('brqhd,brwchd->bhrqwc', qg, kg).astype(F32) * scale + bias[None]
    s_loc = jnp.where(col_ok[:, None, :], s_loc, MASK_VALUE)
    s_ctx = jnp.einsum('brqhd,blhd->bhrql', q_plain.reshape(B, rows, GRID_W, H, dh), k_ctx).astype(F32) * scale
    n_loc = wr * GRID_W
    s = jnp.concatenate([s_loc.reshape(B, H, rows, GRID_W, n_loc), s_ctx], axis=-1)
    p = jax.nn.softmax(s, axis=-1).astype(v.dtype)
    p_loc = p[..., :n_loc].reshape(B, H, rows, GRID_W, wr, GRID_W)
    p_ctx = p[..., n_loc:]
    o = jnp.einsum('bhrqwc,brwchd->brqhd', p_loc, vg) + jnp.einsum('bhrql,blhd->brqhd', p_ctx, v_ctx)
    return o.reshape(B, T, H * dh)


def context_attention(q, k, v):
    B, L, H, dh = q.shape
    s = jnp.einsum('bqhd,bkhd->bhqk', q, k).astype(F32) * dh ** -0.5
    p = jax.nn.softmax(s, axis=-1).astype(v.dtype)
    return jnp.einsum('bhqk,bkhd->bqhd', p, v).reshape(B, L, H * dh)


def merge_branches(ya, yb, yc, gate_raw, w_branch, w_out):
    ys = jnp.stack([ya, yb, yc], axis=-2)
    proj = jnp.einsum('btnw,nwd->btnd', ys, w_branch)
    g = jax.nn.sigmoid(gate_raw.reshape(gate_raw.shape[:-1] + (N_BRANCH, D_MODEL)))
    return jnp.sum(g * proj, axis=-2) @ w_out


def sq_relu_ffn(u, w1, w2):
    return jnp.square(jax.nn.relu(u @ w1)) @ w2


def setup_inputs(seed: int = 0) -> dict:
    key = jax.random.key(seed)
    ks = jax.random.split(key, 26)

    def nrm(k, shape, s):
        return jax.random.normal(k, shape, F32) * s

    a0 = jax.random.uniform(ks[15], (DEPTH, 2, LRU_WIDTH), F32, 0.9, 0.999)
    root = a0 ** (1.0 / LRU_C)
    lru_lambda = jnp.log(root) - jnp.log1p(-root)
    return {
        'x': nrm(ks[0], (BATCH, SEQ, D_MODEL), 1.0),
        'c': nrm(ks[1], (BATCH, D_MODEL), 1.0),
        'ctx': nrm(ks[2], (BATCH, CTX_LEN, D_MODEL), 1.0),
        'c_ctx': nrm(ks[3], (D_MODEL,), 1.0),
        'ada_w': nrm(ks[4], (DEPTH, D_MODEL, 6 * D_MODEL), 0.5 * D_MODEL ** -0.5),
        'ada_b': nrm(ks[5], (DEPTH, 6 * D_MODEL), 0.02),
        'norm1_g': 1.0 + nrm(ks[6], (DEPTH, D_MODEL), 0.02),
        'norm2_g': 1.0 + nrm(ks[7], (DEPTH, D_MODEL), 0.02),
        'w_in': nrm(ks[8], (DEPTH, D_MODEL, D_IN), D_MODEL ** -0.5),
        'conv_w': nrm(ks[9], (DEPTH, CONV_W, LRU_WIDTH), CONV_W ** -0.5),
        'conv_b': nrm(ks[10], (DEPTH, LRU_WIDTH), 0.02),
        'lru_wa': nrm(ks[11], (DEPTH, 2, LRU_BLOCKS, LRU_BLOCK, LRU_BLOCK), LRU_BLOCK ** -0.5),
        'lru_ba': nrm(ks[12], (DEPTH, 2, LRU_WIDTH), 0.02),
        'lru_wx': nrm(ks[13], (DEPTH, 2, LRU_BLOCKS, LRU_BLOCK, LRU_BLOCK), LRU_BLOCK ** -0.5),
        'lru_bx': nrm(ks[14], (DEPTH, 2, LRU_WIDTH), 0.02),
        'lru_lambda': lru_lambda,
        'hg_lb_logits': nrm(ks[16], (2, DEPTH, HG_QK), 0.1),
        'hg_norm_g': 1.0 + nrm(ks[17], (DEPTH, HG_DV), 0.02),
        'na_q_norm_g': 1.0 + nrm(ks[18], (DEPTH, NA_DH), 0.02),
        'na_k_norm_g': 1.0 + nrm(ks[19], (DEPTH, NA_DH), 0.02),
        'na_rpb': nrm(ks[20], (DEPTH, NA_HEADS, 2 * WIN_R - 1, 2 * WIN_C - 1), 0.1),
        'w_branch': nrm(ks[21], (DEPTH, N_BRANCH, BRANCH_WIDTH, D_MODEL), BRANCH_WIDTH ** -0.5),
        'w_out': nrm(ks[22], (DEPTH, D_MODEL, D_MODEL), D_MODEL ** -0.5),
        'ffn_w1': nrm(ks[23], (DEPTH, D_MODEL, D_FF), D_MODEL ** -0.5),
        'ffn_w2': nrm(ks[24], (DEPTH, D_FF, D_MODEL), D_FF ** -0.5),
    }


def reference(x, c, ctx, c_ctx, ada_w, ada_b, norm1_g, norm2_g, w_in, conv_w, conv_b,
              lru_wa, lru_ba, lru_wx, lru_bx, lru_lambda, hg_lb_logits, hg_norm_g,
              na_q_norm_g, na_k_norm_g, na_rpb, w_branch, w_out, ffn_w1, ffn_w2):
    lb_soft = jax.nn.softmax(hg_lb_logits.astype(F32), axis=1)
    lower_bounds = jnp.cumsum(lb_soft, axis=1) - lb_soft[:, :1]
    h, hc = x, ctx
    for l in range(DEPTH):
        ctx_needed = l < DEPTH - 1
        sh1, sc1, gt1, sh2, sc2, gt2 = modulation(c, ada_w[l], ada_b[l])
        sh1c, sc1c, gt1c, sh2c, sc2c, gt2c = modulation(c_ctx, ada_w[l], ada_b[l])
        u = rms_norm(h, norm1_g[l]) * (1.0 + sc1) + sh1
        uc = rms_norm(hc, norm1_g[l]) * (1.0 + sc1c) + sh1c
        a_x, a_g, b_q, b_f, b_i, b_o, c_qkv, m_g = split_projection(u @ w_in[l])
        a_xc, a_gc, b_qc, b_fc, b_ic, b_oc, c_qkvc, m_gc = split_projection(uc @ w_in[l])

        lru_p = (conv_w[l], conv_b[l], lru_wa[l], lru_ba[l], lru_wx[l], lru_bx[l], lru_lambda[l])
        ya_c, lru_state = rglru_branch(a_xc, a_gc, *lru_p, None)
        ya, _ = rglru_branch(a_x, a_g, *lru_p, lru_state)

        lb = lower_bounds[:, l]
        yb_c, hg_state = hgrn2_branch(b_qc, b_fc, b_ic, b_oc, lb, hg_norm_g[l], None)
        yb, _ = hgrn2_branch(b_q, b_f, b_i, b_o, lb, hg_norm_g[l], hg_state)

        q_c, k_c, v_c = na_heads(c_qkvc, na_q_norm_g[l], na_k_norm_g[l])
        q_l, k_l, v_l = na_heads(c_qkv, na_q_norm_g[l], na_k_norm_g[l])
        yc = neighbourhood_attention(axial_rope(q_l), axial_rope(k_l), v_l, q_l, k_c, v_c, na_rpb[l])

        h = h + gt1 * merge_branches(ya, yb, yc, m_g, w_branch[l], w_out[l])
        h = h + gt2 * sq_relu_ffn(rms_norm(h, norm2_g[l]) * (1.0 + sc2) + sh2, ffn_w1[l], ffn_w2[l])
        if ctx_needed:
            yc_c = context_attention(q_c, k_c, v_c)
            hc = hc + gt1c * merge_branches(ya_c, yb_c, yc_c, m_gc, w_branch[l], w_out[l])
            hc = hc + gt2c * sq_relu_ffn(rms_norm(hc, norm2_g[l]) * (1.0 + sc2c) + sh2c, ffn_w1[l], ffn_w2[l])
    return h
```

```python
import functools

import numpy as np
import jax
import jax.numpy as jnp
from jax import lax
from jax.experimental import pallas as pl
from jax.experimental.pallas import tpu as pltpu

F32 = jnp.float32
BF16 = jnp.bfloat16

D = 1024
DEPTH = 2
GRID_W = 64
CTX = 256
BW = 512
LRU_NB = 8
LRU_BS = BW // LRU_NB
LRU_C = 8.0
HG_H = 4
HG_DK = 128
HG_C = 64
NA_H = 8
NA_DH = 64
WIN_R = 8
WIN_C = 16
ROPE_BASE = 10000.0
MASK_VALUE = -1e30
D_FF = 4 * D
EPS = 1e-6
TM = 256
NA_RB = 4
NA_UR = 12
VMEM_LIMIT = 56 * 1024 * 1024


def _cp(sem):
    return pltpu.CompilerParams(dimension_semantics=sem, vmem_limit_bytes=VMEM_LIMIT)


def _mod_row(b, j):
    return jnp.where(j == 0, 8, b)


def _mod_kernel(c_ref, w_ref, b_ref, o_ref):
    c = c_ref[...]
    s = c * jax.nn.sigmoid(c)
    o_ref[0] = jnp.dot(s, w_ref[0], preferred_element_type=F32) + b_ref[0]


def _modulation(cond, ada_w, ada_b):
    tn = 1536
    return pl.pallas_call(
        _mod_kernel,
        grid=(DEPTH, 6 * D // tn),
        in_specs=[
            pl.BlockSpec((16, D), lambda l, n: (0, 0)),
            pl.BlockSpec((1, D, tn), lambda l, n: (l, 0, n)),
            pl.BlockSpec((1, 1, tn), lambda l, n: (l, 0, n)),
        ],
        out_specs=pl.BlockSpec((1, 16, tn), lambda l, n: (l, 0, n)),
        out_shape=jax.ShapeDtypeStruct((DEPTH, 16, 6 * D), F32),
        compiler_params=_cp(("arbitrary", "arbitrary")),
        name="modulation",
    )(cond, ada_w, ada_b.reshape(DEPTH, 1, 6 * D))


def _proj_kernel(x_ref, mod_ref, g_ref, wm_ref, wa_ref, p_ref, pa_ref):
    x = x_ref[0]
    m = mod_ref[0]
    y = x * lax.rsqrt(jnp.mean(x * x, axis=-1, keepdims=True) + EPS) * g_ref[...]
    u = (y * (1.0 + m[:, D:2 * D]) + m[:, 0:D]).astype(BF16)
    p_ref[0] = jnp.dot(u, wm_ref[...], preferred_element_type=F32)
    pa_ref[...] = jnp.dot(u, wa_ref[...], preferred_element_type=F32)


def _projection(h, mod_l, g, w_main, w_a):
    B, S, _ = h.shape
    nt = S // TM
    nm = w_main.shape[1]
    return pl.pallas_call(
        _proj_kernel,
        grid=(B, nt),
        in_specs=[
            pl.BlockSpec((1, TM, D), lambda b, j: (b, j, 0)),
            pl.BlockSpec((1, 1, 6 * D), lambda b, j: (_mod_row(b, j), 0, 0)),
            pl.BlockSpec((1, D), lambda b, j: (0, 0)),
            pl.BlockSpec((D, nm), lambda b, j: (0, 0), pipeline_mode=pl.Buffered(1)),
            pl.BlockSpec((D, 2 * BW), lambda b, j: (0, 0), pipeline_mode=pl.Buffered(1)),
        ],
        out_specs=[
            pl.BlockSpec((1, TM, nm), lambda b, j: (b, j, 0)),
            pl.BlockSpec((TM, 2 * BW), lambda b, j: (j, b)),
        ],
        out_shape=[
            jax.ShapeDtypeStruct((B, S, nm), F32),
            jax.ShapeDtypeStruct((S, B * 2 * BW), F32),
        ],
        compiler_params=_cp(("arbitrary", "arbitrary")),
        name="projection",
    )(h, mod_l, g, w_main, w_a)


LRU_CW = 256
LRU_TM = 128


def _seq_tile(p, j, nt, nct):
    return jnp.where(p == 0, j, jnp.where(j < nct, nct - 1 - j, nt - 1 - (j - nct)))


def _softplus(x):
    return jnp.maximum(x, 0.0) + jnp.log1p(jnp.exp(-jnp.abs(x)))


def _gelu_tanh(x):
    return 0.5 * x * (1.0 + jnp.tanh(np.sqrt(2.0 / np.pi) * (x + 0.044715 * (x * x * x))))


def _lru_kernel(x_ref, gate_ref, prev_ref, next_ref, cw_ref, cb_ref, wa_ref, wx_ref, ba_ref, bx_ref, lam_ref,
                y_ref, hf_ref, a_ref, b_ref, carry_ref, *, nt, nct):
    TM = LRU_TM
    p = pl.program_id(1)
    j = pl.program_id(2)
    tile = _seq_tile(p, j, nt, nct)
    nb = x_ref.shape[1]

    @pl.when(j == 0)
    def _():
        carry_ref[...] = jnp.zeros_like(carry_ref)

    x = x_ref[...]
    has_prev = jnp.logical_and(tile != 0, tile != nct)
    has_next = jnp.logical_and(tile != nct - 1, tile != nt - 1)
    prev = jnp.where(has_prev, prev_ref[...], 0.0)
    nxt = jnp.where(has_next, next_ref[...], 0.0)
    xe = jnp.concatenate([prev, x, nxt], axis=0)
    cw = cw_ref[...]
    xc = cb_ref[...][None] + xe[0:TM] * cw[0:1][None] + xe[1:TM + 1] * cw[1:2][None] \
        + xe[2:TM + 2] * cw[2:3][None] + xe[3:TM + 3] * cw[3:4][None]
    xc2 = xc.reshape(TM * nb, LRU_CW)
    xb = xc2.astype(BF16)
    r = jax.nn.sigmoid(jnp.dot(xb, wa_ref[0, 0], preferred_element_type=F32) + ba_ref[0])
    i = jax.nn.sigmoid(jnp.dot(xb, wx_ref[0, 0], preferred_element_type=F32) + bx_ref[0])
    log_a = (-LRU_C) * r * _softplus(-lam_ref[0])
    a = jnp.exp(log_a)
    bc = jnp.sqrt(1.0 - a * a) * (i * xc2)
    a_ref[...] = a.reshape(TM, nb, LRU_CW)
    b_ref[...] = bc.reshape(TM, nb, LRU_CW)

    def step(t, h):
        h = a_ref[t] * h + b_ref[t]
        b_ref[t] = h
        return h

    @pl.when(p == 0)
    def _():
        carry_ref[...] = lax.fori_loop(0, TM, step, carry_ref[...], unroll=8)
        hf_ref[pl.ds(pl.multiple_of(tile * TM, TM), TM)] = b_ref[...]

    @pl.when(p == 1)
    def _():
        carry_ref[...] = lax.fori_loop(0, TM, lambda s, h: step(TM - 1 - s, h), carry_ref[...], unroll=8)
        hf = hf_ref[pl.ds(pl.multiple_of(tile * TM, TM), TM)]
        y_ref[...] = (hf + b_ref[...]) * _gelu_tanh(gate_ref[...])


def _rglru(pa, conv_w, conv_b, wa_bd, wx_bd, ba, bx, lam):
    S, B, _ = pa.shape
    TM = LRU_TM
    nt = S // TM
    nct = CTX // TM
    ncw = BW // LRU_CW
    tile = functools.partial(_seq_tile, nt=nt, nct=nct)
    vec = lambda: pl.BlockSpec((1, 1, LRU_CW), lambda c, p, j: (p, 0, c))
    return pl.pallas_call(
        functools.partial(_lru_kernel, nt=nt, nct=nct),
        grid=(ncw, 2, nt),
        in_specs=[
            pl.BlockSpec((TM, B, LRU_CW), lambda c, p, j: (tile(p, j), 0, c)),
            pl.BlockSpec((TM, B, LRU_CW), lambda c, p, j: (tile(p, j), 0, ncw + c)),
            pl.BlockSpec((2, B, LRU_CW), lambda c, p, j: (jnp.maximum(tile(p, j) * (TM // 2) - 1, 0), 0, c)),
            pl.BlockSpec((1, B, LRU_CW), lambda c, p, j: (jnp.minimum((tile(p, j) + 1) * TM, S - 1), 0, c)),
            pl.BlockSpec((4, LRU_CW), lambda c, p, j: (0, c)),
            pl.BlockSpec((1, LRU_CW), lambda c, p, j: (0, c)),
            pl.BlockSpec((1, 1, LRU_CW, LRU_CW), lambda c, p, j: (p, c, 0, 0)),
            pl.BlockSpec((1, 1, LRU_CW, LRU_CW), lambda c, p, j: (p, c, 0, 0)),
            vec(), vec(), vec(),
        ],
        out_specs=pl.BlockSpec((TM, B, LRU_CW), lambda c, p, j: (jnp.where(p == 0, nct - 1, tile(p, j)), 0, c)),
        out_shape=jax.ShapeDtypeStruct((S, B, BW), F32),
        scratch_shapes=[
            pltpu.VMEM((S, B, LRU_CW), F32),
            pltpu.VMEM((TM, B, LRU_CW), F32),
            pltpu.VMEM((TM, B, LRU_CW), F32),
            pltpu.VMEM((B, LRU_CW), F32),
        ],
        compiler_params=_cp(("arbitrary", "arbitrary", "arbitrary")),
        name="rglru",
    )(pa, pa, pa, pa, conv_w, conv_b, wa_bd, wx_bd, ba, bx, lam)


def _block_diag(w):
    per = LRU_CW // LRU_BS
    w = w.reshape(2, BW // LRU_CW, per, LRU_BS, LRU_BS)
    eye = jnp.eye(per, dtype=w.dtype)
    out = jnp.einsum('dcnio,nm->dcnimo', w, eye)
    return out.reshape(2, BW // LRU_CW, LRU_CW, LRU_CW)


HG_NLEV = 6
HG_ROWS = (HG_NLEV + 2) * HG_C


def _hgrn_constants():
    C = HG_C
    W = np.zeros((2, HG_NLEV + 2, C, C), np.float32)
    M = np.zeros((2, HG_NLEV + 1, C, C), np.float32)
    for lv in range(HG_NLEV):
        m = C >> (lv + 1)
        for t in range(C):
            blk = t // m
            if blk % 2 == 1:
                p = blk * m
                W[0, lv, t, p:t + 1] = 1.0
                W[1, lv, t, p:t] = 1.0
            else:
                p = (blk + 1) * m
                W[0, lv, t, t + 1:p] = 1.0
                W[1, lv, t, t:p] = 1.0
        for t in range(C):
            for s in range(C):
                if (t // m) % 2 == 1 and s // m == t // m - 1:
                    M[0, lv, t, s] = 1.0
                    M[1, lv, s, t] = 1.0
    for t in range(C):
        W[0, HG_NLEV, t, :t + 1] = 1.0
        W[1, HG_NLEV, t, t:] = 1.0
        W[0, HG_NLEV + 1, t, t + 1:] = 1.0
        W[1, HG_NLEV + 1, t, :t] = 1.0
        M[:, HG_NLEV, t, t] = 1.0
    W = W.reshape(2, HG_ROWS, C)
    return np.concatenate([W, W, W], axis=-1), M


def _hgrn_kernel(q_ref, f_ref, v_ref, og_ref, lbl_ref, ng_ref, w_ref, m_ref, y_ref, of_ref, st_ref, *, nt, layer):
    p = pl.program_id(1)
    j = pl.program_id(2)
    tile = _seq_tile(p, j, nt, CTX // TM)
    C = HG_C
    nchunk = TM // C

    @pl.when(j == 0)
    def _():
        st_ref[...] = jnp.zeros_like(st_ref)

    logits = lbl_ref[0]
    e = jnp.exp(logits - jnp.max(logits, axis=0, keepdims=True))
    soft = e / jnp.sum(e, axis=0, keepdims=True)
    lb = jnp.sum(soft[0:layer + 1], axis=0, keepdims=True) - soft[0:1]

    wmat = w_ref[0]
    masks = m_ref[0]
    row0 = pl.multiple_of(tile * TM, TM)

    def chunk(c0, backward):
        sl = slice(c0, c0 + C)
        f = lb + (1.0 - lb) * jax.nn.sigmoid(f_ref[0, sl, :])
        g = jnp.log(f)
        k = 1.0 - f
        qr = q_ref[0, sl, :]
        q = qr * jax.nn.sigmoid(qr)
        v = v_ref[0, sl, :]
        g1 = g.astype(BF16)
        r1 = g - g1.astype(F32)
        g2 = r1.astype(BF16)
        g3 = (r1 - g2.astype(F32)).astype(BF16)
        g_split = jnp.concatenate([g1, g2, g3], axis=0)
        ex = jnp.exp(jnp.dot(wmat, g_split, preferred_element_type=F32))
        outs = []
        for h in range(HG_H):
            hs = slice(h * HG_DK, (h + 1) * HG_DK)
            qh, kh = q[:, hs], k[:, hs]
            vh = v[:, hs].astype(BF16)
            eh = ex[:, hs].reshape(HG_NLEV + 2, C, HG_DK)
            qs = jnp.concatenate([qh[None] * eh[:HG_NLEV], qh[None]], axis=0).astype(BF16)
            ks = jnp.concatenate([kh[None] * eh[:HG_NLEV], kh[None]], axis=0).astype(BF16)
            att_l = jnp.einsum('lqd,lkd->lqk', qs, ks, preferred_element_type=F32)
            att = jnp.sum(jnp.where(masks > 0.5, att_l, 0.0), axis=0)
            st = st_ref[h]
            q_in = (qh * eh[HG_NLEV]).astype(BF16)
            o = jnp.dot(att.astype(BF16), vh, preferred_element_type=F32) \
                + lax.dot_general(q_in, st.astype(BF16), (((1,), (1,)), ((), ())), preferred_element_type=F32)
            dec = eh[HG_NLEV][0:1] if backward else eh[HG_NLEV][C - 1:C]
            k_st = (kh * eh[HG_NLEV + 1]).astype(BF16)
            st_ref[h] = dec * st + lax.dot_general(vh, k_st, (((0,), (0,)), ((), ())), preferred_element_type=F32)
            outs.append(o)
        return jnp.concatenate(outs, axis=-1)

    @pl.when(p == 0)
    def _():
        for ci in range(nchunk):
            o = chunk(ci * C, False)
            of_ref[pl.ds(row0 + ci * C, C), :] = o

    @pl.when(p == 1)
    def _():
        for ci in reversed(range(nchunk)):
            o = chunk(ci * C, True) + of_ref[pl.ds(row0 + ci * C, C), :]
            ys = []
            for h in range(HG_H):
                oh = o[:, h * HG_DK:(h + 1) * HG_DK]
                ys.append(oh * lax.rsqrt(jnp.mean(oh * oh, axis=-1, keepdims=True) + EPS))
            yn = jnp.concatenate(ys, axis=-1) * ng_ref[...]
            y_ref[0, ci * C:(ci + 1) * C, :] = yn * jax.nn.sigmoid(og_ref[0, ci * C:(ci + 1) * C, :])


def _hgrn2(pm, lb_logits, norm_g, layer):
    B, S, _ = pm.shape
    nt = S // TM
    tile = functools.partial(_seq_tile, nt=nt, nct=CTX // TM)
    w_np, m_np = _hgrn_constants()
    w_c = jnp.asarray(w_np, BF16)
    m_c = jnp.asarray(m_np, F32)
    col = lambda cb: pl.BlockSpec((1, TM, BW), lambda b, p, j: (b, tile(p, j), cb))
    return pl.pallas_call(
        functools.partial(_hgrn_kernel, nt=nt, layer=layer),
        grid=(B, 2, nt),
        in_specs=[
            col(0),
            pl.BlockSpec((1, TM, BW), lambda b, p, j: (b, tile(p, j), 1 + p)),
            col(3), col(4),
            pl.BlockSpec((1, DEPTH, BW), lambda b, p, j: (p, 0, 0)),
            pl.BlockSpec((1, BW), lambda b, p, j: (0, 0)),
            pl.BlockSpec((1, HG_ROWS, 3 * HG_C), lambda b, p, j: (p, 0, 0)),
            pl.BlockSpec((1, HG_NLEV + 1, HG_C, HG_C), lambda b, p, j: (p, 0, 0, 0)),
        ],
        out_specs=pl.BlockSpec((1, TM, BW), lambda b, p, j: (b, jnp.where(p == 0, CTX // TM - 1, tile(p, j)), 0)),
        out_shape=jax.ShapeDtypeStruct((B, S, BW), F32),
        scratch_shapes=[
            pltpu.VMEM((S, BW), F32),
            pltpu.VMEM((HG_H, HG_DK, HG_DK), F32),
        ],
        compiler_params=_cp(("arbitrary", "arbitrary", "arbitrary")),
        name="hgrn2",
    )(pm, pm, pm, pm, lb_logits, norm_g, w_c, m_c)


def _seg_mean_sq(x, seg):
    x2 = x * x
    hi = x2.astype(BF16)
    lo = (x2 - hi.astype(F32)).astype(BF16)
    s = jnp.dot(hi, seg, preferred_element_type=F32) + jnp.dot(lo, seg, preferred_element_type=F32)
    return s * (1.0 / NA_DH)


def _swap16(y):
    n = y.shape[-1]
    lane = lax.broadcasted_iota(jnp.int32, y.shape, 1)
    return jnp.where(lane % 32 < 16, pltpu.roll(y, n - 16, 1), pltpu.roll(y, 16, 1))


def _qkv_kernel(q_ref, k_ref, v_ref, qg_ref, kg_ref, seg_ref, cos_ref, sin_ref, qr_ref, qp_ref, kr_ref, vo_ref):
    seg = seg_ref[...]
    cos = cos_ref[...]
    sin = sin_ref[...]
    scale = NA_DH ** -0.5
    q = q_ref[0]
    qn = q * lax.rsqrt(_seg_mean_sq(q, seg) + EPS) * qg_ref[...]
    qp_ref[0] = (qn * scale).astype(BF16)
    qr_ref[0] = ((qn * cos + _swap16(qn) * sin) * scale).astype(BF16)
    k = k_ref[0]
    kn = k * lax.rsqrt(_seg_mean_sq(k, seg) + EPS) * kg_ref[...]
    kr_ref[0] = (kn * cos + _swap16(kn) * sin).astype(BF16)
    vo_ref[0] = v_ref[0].astype(BF16)


def _rope_tables(S):
    half = NA_DH // 2
    nf = half // 2
    t = np.arange(S - CTX)
    inv_freq = (ROPE_BASE ** (-np.arange(nf, dtype=np.float32) / nf)).astype(np.float32)
    ang_r = (t // GRID_W).astype(np.float32)[:, None] * inv_freq
    ang_c = (t % GRID_W).astype(np.float32)[:, None] * inv_freq
    ang = np.concatenate([ang_r, ang_r, ang_c, ang_c], axis=-1)
    sign = np.concatenate([-np.ones(nf), np.ones(nf), -np.ones(nf), np.ones(nf)]).astype(np.float32)
    cos = np.concatenate([np.ones((CTX, NA_DH), np.float32), np.cos(ang)], axis=0)
    sin = np.concatenate([np.zeros((CTX, NA_DH), np.float32), np.sin(ang) * sign], axis=0)
    return np.tile(cos, (1, NA_H)).astype(np.float32), np.tile(sin, (1, NA_H)).astype(np.float32)


def _qkv_prep(pm, q_g, k_g):
    B, S, _ = pm.shape
    nt = S // TM
    cos, sin = _rope_tables(S)
    seg = np.kron(np.eye(NA_H, dtype=np.float32), np.ones((NA_DH, NA_DH), np.float32))
    col = lambda cb: pl.BlockSpec((1, TM, BW), lambda b, j: (b, j, cb))
    full = lambda shape: pl.BlockSpec(shape, lambda b, j: (0,) * len(shape))
    out = pl.BlockSpec((1, TM, BW), lambda b, j: (b, j, 0))
    return pl.pallas_call(
        _qkv_kernel,
        grid=(B, nt),
        in_specs=[col(5), col(6), col(7), full((1, BW)), full((1, BW)), full((BW, BW)),
                  pl.BlockSpec((TM, BW), lambda b, j: (j, 0)), pl.BlockSpec((TM, BW), lambda b, j: (j, 0))],
        out_specs=[out, out, out, out],
        out_shape=[jax.ShapeDtypeStruct((B, S, BW), BF16)] * 4,
        compiler_params=_cp(("arbitrary", "arbitrary")),
        name="qkv_prep",
    )(pm, pm, pm, q_g, k_g, jnp.asarray(seg, BF16), jnp.asarray(cos), jnp.asarray(sin))


NA_GH = 4
NA_GW = NA_GH * NA_DH


def _head_mask(h, n_rows):
    lane = lax.broadcasted_iota(jnp.int32, (n_rows, NA_GW), 1)
    return lane // NA_DH == h


def _softmax_pv(s_parts, v_parts):
    m = s_parts[0].max(axis=-1, keepdims=True)
    for s in s_parts[1:]:
        m = jnp.maximum(m, s.max(axis=-1, keepdims=True))
    ps = [jnp.exp(s - m) for s in s_parts]
    den = ps[0].sum(axis=-1, keepdims=True)
    for pp in ps[1:]:
        den = den + pp.sum(axis=-1, keepdims=True)
    inv = 1.0 / den
    ps = [(pp * inv).astype(BF16) for pp in ps]
    acc = jnp.zeros((TM, NA_GW), F32)
    for h in range(NA_GH):
        for pp, vv in zip(ps, v_parts):
            vm = jnp.where(_head_mask(h, vv.shape[0]), vv, jnp.zeros_like(vv))
            acc = acc + jnp.dot(pp[h * TM:(h + 1) * TM], vm, preferred_element_type=F32)
    return acc.astype(BF16)


def _na_kernel(qr_ref, qp_ref, k0_ref, k1_ref, k2_ref, v0_ref, v1_ref, v2_ref, kc_ref, vc_ref, bias_ref, y_ref,
               *, with_ctx):
    step = pl.program_id(0)
    nkeys = NA_UR * GRID_W
    masks_q = [_head_mask(h, TM) for h in range(NA_GH)]

    def stack_heads(q):
        return jnp.concatenate([jnp.where(masks_q[h], q, jnp.zeros_like(q)) for h in range(NA_GH)], axis=0)

    nt_dims = (((1,), (1,)), ((), ()))

    def latent():
        for g in range(NA_H // NA_GH):
            gs = slice(g * NA_GW, (g + 1) * NA_GW)
            kw = jnp.concatenate([k0_ref[0, :, gs], k1_ref[0, :, gs], k2_ref[0, :, gs]], axis=0)
            vw = jnp.concatenate([v0_ref[0, :, gs], v1_ref[0, :, gs], v2_ref[0, :, gs]], axis=0)
            s_loc = lax.dot_general(stack_heads(qr_ref[0, :, gs]), kw, nt_dims, preferred_element_type=F32)
            s_loc = s_loc + bias_ref[0, g * NA_GH:(g + 1) * NA_GH].reshape(NA_GH * TM, nkeys)
            s_ctx = lax.dot_general(stack_heads(qp_ref[0, :, gs]), kc_ref[0, :, gs], nt_dims,
                                    preferred_element_type=F32)
            y_ref[0, :, gs] = _softmax_pv([s_loc, s_ctx], [vw, vc_ref[0, :, gs]])

    def context():
        for g in range(NA_H // NA_GH):
            gs = slice(g * NA_GW, (g + 1) * NA_GW)
            s_ctx = lax.dot_general(stack_heads(qp_ref[0, :, gs]), kc_ref[0, :, gs], nt_dims,
                                    preferred_element_type=F32)
            y_ref[0, :, gs] = _softmax_pv([s_ctx], [vc_ref[0, :, gs]])

    if with_ctx:
        pl.when(step == 0)(context)
        pl.when(step > 0)(latent)
    else:
        latent()


def _na_bias_tables(rpb):
    rows = 2048 // GRID_W
    col = np.arange(GRID_W)
    c_start = np.clip(col - WIN_C // 2, 0, GRID_W - WIN_C)
    col_ok = (col[None, :] >= c_start[:, None]) & (col[None, :] < c_start[:, None] + WIN_C)
    d_col = np.clip(col[None, :] - col[:, None], -(WIN_C - 1), WIN_C - 1) + (WIN_C - 1)
    tabs = []
    for rb in (0, 1, rows // NA_RB - 1):
        r = rb * NA_RB + np.arange(NA_RB)
        u0 = int(np.clip(rb * NA_RB - NA_RB, 0, rows - NA_UR))
        kr = u0 + np.arange(NA_UR)
        kr0 = np.clip(r - WIN_R // 2, 0, rows - WIN_R)
        row_ok = (kr[None, :] >= kr0[:, None]) & (kr[None, :] < kr0[:, None] + WIN_R)
        d_row = np.clip(kr[None, :] - r[:, None] + (WIN_R - 1), 0, 2 * WIN_R - 2)
        ok = row_ok[:, None, :, None] & col_ok[None, :, None, :]
        bias = rpb.astype(F32)[:, d_row[:, None, :, None], d_col[None, :, None, :]]
        bias = jnp.where(jnp.asarray(ok)[None], bias, MASK_VALUE)
        tabs.append(bias.reshape(NA_H, NA_RB * GRID_W, NA_UR * GRID_W))
    return jnp.stack(tabs)


def _na_key_base(step_latent):
    return jnp.clip(step_latent - 1, 0, 2048 // TM - NA_UR * GRID_W // TM)


def _attention(qr, qp, kr, vv, bias, with_ctx):
    B, S, _ = qr.shape
    nlat = (S - CTX) // TM
    off = 0 if with_ctx else 1

    def qtile(t, b):
        return (b, t + off, 0)

    def ktile(i):
        def f(t, b):
            return (b, 1 + _na_key_base(t + off - 1) + i, 0)
        return f

    def bias_idx(t, b):
        lat = t + off - 1
        return (jnp.where(lat <= 0, 0, jnp.where(lat == nlat - 1, 2, 1)), 0, 0, 0)

    blk = lambda f: pl.BlockSpec((1, TM, BW), f)
    return pl.pallas_call(
        functools.partial(_na_kernel, with_ctx=with_ctx),
        grid=(nlat + (1 if with_ctx else 0), B),
        in_specs=[blk(qtile), blk(qtile), blk(ktile(0)), blk(ktile(1)), blk(ktile(2)),
                  blk(ktile(0)), blk(ktile(1)), blk(ktile(2)),
                  blk(lambda t, b: (b, 0, 0)), blk(lambda t, b: (b, 0, 0)),
                  pl.BlockSpec((1, NA_H, TM, NA_UR * GRID_W), bias_idx)],
        out_specs=blk(qtile),
        out_shape=jax.ShapeDtypeStruct((B, S, BW), BF16),
        compiler_params=_cp(("arbitrary", "arbitrary")),
        name="attention",
    )(qr, qp, kr, kr, kr, vv, vv, vv, kr, vv, bias)


def _merge_kernel(h_ref, mod_ref, ya_ref, yb_ref, yc_ref, g0_ref, g1_ref, g2_ref, wb_ref, wo_ref, o_ref):
    m = mod_ref[0]
    acc = None
    for n, (y_ref, g_ref) in enumerate(((ya_ref, g0_ref), (yb_ref, g1_ref), (yc_ref, g2_ref))):
        y = y_ref[...].reshape(TM, BW).astype(BF16)
        term = jax.nn.sigmoid(g_ref[0]) * jnp.dot(y, wb_ref[n], preferred_element_type=F32)
        acc = term if acc is None else acc + term
    out = jnp.dot(acc.astype(BF16), wo_ref[...], preferred_element_type=F32)
    o_ref[0] = h_ref[0] + m[:, 2 * D:3 * D] * out


def _merge(h, mod_l, ya, yb, yc, pm, w_branch, w_out, first_tile):
    B, S, _ = h.shape
    nt = S // TM - first_tile
    tok = lambda w: pl.BlockSpec((1, TM, w), lambda b, j: (b, j + first_tile, 0))
    gate = lambda n: pl.BlockSpec((1, TM, D), lambda b, j: (b, j + first_tile, 4 + n))
    return pl.pallas_call(
        _merge_kernel,
        grid=(B, nt),
        in_specs=[
            tok(D),
            pl.BlockSpec((1, 1, 6 * D), lambda b, j: (_mod_row(b, j + first_tile), 0, 0)),
            pl.BlockSpec((TM, BW), lambda b, j: (j + first_tile, b)),
            tok(BW), tok(BW), gate(0), gate(1), gate(2),
            pl.BlockSpec((3, BW, D), lambda b, j: (0, 0, 0)),
            pl.BlockSpec((D, D), lambda b, j: (0, 0)),
        ],
        out_specs=pl.BlockSpec((1, TM, D), lambda b, j: (b, j, 0)),
        out_shape=jax.ShapeDtypeStruct((B, nt * TM, D), F32),
        compiler_params=_cp(("arbitrary", "arbitrary")),
        name="merge",
    )(h, mod_l, ya, yb, yc, pm, pm, pm, w_branch, w_out)


def _ffn_kernel(h_ref, mod_ref, g_ref, w1_ref, w2_ref, o_ref):
    x = h_ref[0]
    m = mod_ref[0]
    y = x * lax.rsqrt(jnp.mean(x * x, axis=-1, keepdims=True) + EPS) * g_ref[...]
    u = (y * (1.0 + m[:, 4 * D:5 * D]) + m[:, 3 * D:4 * D]).astype(BF16)
    a = jnp.maximum(jnp.dot(u, w1_ref[...], preferred_element_type=F32), 0.0)
    out = jnp.dot((a * a).astype(BF16), w2_ref[...], preferred_element_type=F32)
    o_ref[0] = x + m[:, 5 * D:6 * D] * out


def _ffn(h, mod_l, g, w1, w2, first_tile):
    B, n, _ = h.shape
    return pl.pallas_call(
        _ffn_kernel,
        grid=(B, n // TM),
        in_specs=[
            pl.BlockSpec((1, TM, D), lambda b, j: (b, j, 0)),
            pl.BlockSpec((1, 1, 6 * D), lambda b, j: (_mod_row(b, j + first_tile), 0, 0)),
            pl.BlockSpec((1, D), lambda b, j: (0, 0)),
            pl.BlockSpec((D, D_FF), lambda b, j: (0, 0), pipeline_mode=pl.Buffered(1)),
            pl.BlockSpec((D_FF, D), lambda b, j: (0, 0), pipeline_mode=pl.Buffered(1)),
        ],
        out_specs=pl.BlockSpec((1, TM, D), lambda b, j: (b, j, 0)),
        out_shape=jax.ShapeDtypeStruct((B, n, D), F32),
        compiler_params=_cp(("arbitrary", "arbitrary")),
        name="ffn",
    )(h, mod_l, g, w1, w2)


def kernel(x, c, ctx, c_ctx, ada_w, ada_b, norm1_g, norm2_g, w_in, conv_w, conv_b, lru_wa, lru_ba, lru_wx, lru_bx,
           lru_lambda, hg_lb_logits, hg_norm_g, na_q_norm_g, na_k_norm_g, na_rpb, w_branch, w_out, ffn_w1, ffn_w2):
    B, T, _ = x.shape
    S = CTX + T
    cond = jnp.zeros((16, D), F32).at[:B].set(c).at[8].set(c_ctx)
    mod = _modulation(cond, ada_w, ada_b)
    h = jnp.concatenate([ctx, x], axis=1)
    for l in range(DEPTH):
        last = l == DEPTH - 1
        mod_l = mod[l].reshape(16, 1, 6 * D)
        w_l = w_in[l].astype(BF16)
        pm, pa = _projection(h, mod_l, norm1_g[l][None], w_l[:, 2 * BW:], w_l[:, :2 * BW])

        ya = _rglru(pa.reshape(S, B, 2 * BW), conv_w[l], conv_b[l][None],
                    _block_diag(lru_wa[l]).astype(BF16), _block_diag(lru_wx[l]).astype(BF16),
                    lru_ba[l][:, None], lru_bx[l][:, None], lru_lambda[l][:, None])
        yb = _hgrn2(pm, hg_lb_logits, jnp.tile(hg_norm_g[l], HG_H)[None], l)
        qr, qp, kr, vv = _qkv_prep(pm, jnp.tile(na_q_norm_g[l], NA_H)[None], jnp.tile(na_k_norm_g[l], NA_H)[None])
        yc = _attention(qr, qp, kr, vv, _na_bias_tables(na_rpb[l]), with_ctx=not last)

        first = 1 if last else 0
        h1 = _merge(h, mod_l, ya.reshape(S, B * BW), yb, yc, pm, w_branch[l].astype(BF16), w_out[l].astype(BF16),
                    first)
        h = _ffn(h1, mod_l, norm2_g[l][None], ffn_w1[l].astype(BF16), ffn_w2[l].astype(BF16), first)
    return h
```

```python
import functools

import numpy as np
import jax
import jax.numpy as jnp
from jax import lax
from jax.experimental import pallas as pl
from jax.experimental.pallas import tpu as pltpu

F32 = jnp.float32
BF16 = jnp.bfloat16

D = 1024
DEPTH = 2
GRID_W = 64
CTX = 256
BW = 512
LRU_NB = 8
LRU_BS = BW // LRU_NB
LRU_C = 8.0
HG_H = 4
HG_DK = 128
HG_C = 64
NA_H = 8
NA_DH = 64
WIN_R = 8
WIN_C = 16
ROPE_BASE = 10000.0
MASK_VALUE = -1e30
D_FF = 4 * D
EPS = 1e-6
TM = 256
NA_RB = 4
NA_UR = 12
VMEM_LIMIT = 56 * 1024 * 1024


def _cp(sem):
    return pltpu.CompilerParams(dimension_semantics=sem, vmem_limit_bytes=VMEM_LIMIT)


def _mod_row(b, j):
    return jnp.where(j == 0, 8, b)


def _mod_kernel(c_ref, w_ref, b_ref, o_ref):
    c = c_ref[...]
    s = c * jax.nn.sigmoid(c)
    o_ref[0] = jnp.dot(s, w_ref[0], preferred_element_type=F32) + b_ref[0]


def _modulation(cond, ada_w, ada_b):
    tn = 1536
    return pl.pallas_call(
        _mod_kernel,
        grid=(DEPTH, 6 * D // tn),
        in_specs=[
            pl.BlockSpec((16, D), lambda l, n: (0, 0)),
            pl.BlockSpec((1, D, tn), lambda l, n: (l, 0, n)),
            pl.BlockSpec((1, 1, tn), lambda l, n: (l, 0, n)),
        ],
        out_specs=pl.BlockSpec((1, 16, tn), lambda l, n: (l, 0, n)),
        out_shape=jax.ShapeDtypeStruct((DEPTH, 16, 6 * D), F32),
        compiler_params=_cp(("arbitrary", "arbitrary")),
        name="modulation",
    )(cond, ada_w, ada_b.reshape(DEPTH, 1, 6 * D))


def _proj_kernel(x_ref, mod_ref, g_ref, wm_ref, wa_ref, p_ref, pa_ref):
    x = x_ref[0]
    m = mod_ref[0]
    y = x * lax.rsqrt(jnp.mean(x * x, axis=-1, keepdims=True) + EPS) * g_ref[...]
    u = (y * (1.0 + m[:, D:2 * D]) + m[:, 0:D]).astype(BF16)
    p_ref[0] = jnp.dot(u, wm_ref[...], preferred_element_type=F32)
    pa_ref[...] = jnp.dot(u, wa_ref[...], preferred_element_type=F32)


def _projection(h, mod_l, g, w_main, w_a):
    B, S, _ = h.shape
    nt = S // TM
    nm = w_main.shape[1]
    return pl.pallas_call(
        _proj_kernel,
        grid=(B, nt),
        in_specs=[
            pl.BlockSpec((1, TM, D), lambda b, j: (b, j, 0)),
            pl.BlockSpec((1, 1, 6 * D), lambda b, j: (_mod_row(b, j), 0, 0)),
            pl.BlockSpec((1, D), lambda b, j: (0, 0)),
            pl.BlockSpec((D, nm), lambda b, j: (0, 0), pipeline_mode=pl.Buffered(1)),
            pl.BlockSpec((D, 2 * BW), lambda b, j: (0, 0), pipeline_mode=pl.Buffered(1)),
        ],
        out_specs=[
            pl.BlockSpec((1, TM, nm), lambda b, j: (b, j, 0)),
            pl.BlockSpec((TM, 2 * BW), lambda b, j: (j, b)),
        ],
        out_shape=[
            jax.ShapeDtypeStruct((B, S, nm), F32),
            jax.ShapeDtypeStruct((S, B * 2 * BW), F32),
        ],
        compiler_params=_cp(("arbitrary", "arbitrary")),
        name="projection",
    )(h, mod_l, g, w_main, w_a)


LRU_CW = 256
LRU_TM = 128


def _seq_tile(p, j, nt, nct):
    return jnp.where(p == 0, j, jnp.where(j < nct, nct - 1 - j, nt - 1 - (j - nct)))


def _softplus(x):
    return jnp.maximum(x, 0.0) + jnp.log1p(jnp.exp(-jnp.abs(x)))


def _gelu_tanh(x):
    return 0.5 * x * (1.0 + jnp.tanh(np.sqrt(2.0 / np.pi) * (x + 0.044715 * (x * x * x))))


def _lru_kernel(x_ref, gate_ref, prev_ref, next_ref, cw_ref, cb_ref, wa_ref, wx_ref, ba_ref, bx_ref, lam_ref,
                y_ref, hf_ref, a_ref, b_ref, carry_ref, *, nt, nct):
    TM = LRU_TM
    p = pl.program_id(1)
    j = pl.program_id(2)
    tile = _seq_tile(p, j, nt, nct)
    nb = x_ref.shape[1]

    @pl.when(j == 0)
    def _():
        carry_ref[...] = jnp.zeros_like(carry_ref)

    x = x_ref[...]
    has_prev = jnp.logical_and(tile != 0, tile != nct)
    has_next = jnp.logical_and(tile != nct - 1, tile != nt - 1)
    prev = jnp.where(has_prev, prev_ref[...], 0.0)
    nxt = jnp.where(has_next, next_ref[...], 0.0)
    xe = jnp.concatenate([prev, x, nxt], axis=0)
    cw = cw_ref[...]
    xc = cb_ref[...][None] + xe[0:TM] * cw[0:1][None] + xe[1:TM + 1] * cw[1:2][None] \
        + xe[2:TM + 2] * cw[2:3][None] + xe[3:TM + 3] * cw[3:4][None]
    xc2 = xc.reshape(TM * nb, LRU_CW)
    xb = xc2.astype(BF16)
    r = jax.nn.sigmoid(jnp.dot(xb, wa_ref[0, 0], preferred_element_type=F32) + ba_ref[0])
    i = jax.nn.sigmoid(jnp.dot(xb, wx_ref[0, 0], preferred_element_type=F32) + bx_ref[0])
    log_a = (-LRU_C) * r * _softplus(-lam_ref[0])
    a = jnp.exp(log_a)
    bc = jnp.sqrt(1.0 - a * a) * (i * xc2)
    a_ref[...] = a.reshape(TM, nb, LRU_CW)
    b_ref[...] = bc.reshape(TM, nb, LRU_CW)

    def step(t, h):
        h = a_ref[t] * h + b_ref[t]
        b_ref[t] = h
        return h

    @pl.when(p == 0)
    def _():
        carry_ref[...] = lax.fori_loop(0, TM, step, carry_ref[...], unroll=8)
        hf_ref[pl.ds(pl.multiple_of(tile * TM, TM), TM)] = b_ref[...]

    @pl.when(p == 1)
    def _():
        carry_ref[...] = lax.fori_loop(0, TM, lambda s, h: step(TM - 1 - s, h), carry_ref[...], unroll=8)
        hf = hf_ref[pl.ds(pl.multiple_of(tile * TM, TM), TM)]
        y_ref[...] = (hf + b_ref[...]) * _gelu_tanh(gate_ref[...])


def _rglru(pa, conv_w, conv_b, wa_bd, wx_bd, ba, bx, lam):
    S, B, _ = pa.shape
    TM = LRU_TM
    nt = S // TM
    nct = CTX // TM
    ncw = BW // LRU_CW
    tile = functools.partial(_seq_tile, nt=nt, nct=nct)
    vec = lambda: pl.BlockSpec((1, 1, LRU_CW), lambda c, p, j: (p, 0, c))
    return pl.pallas_call(
        functools.partial(_lru_kernel, nt=nt, nct=nct),
        grid=(ncw, 2, nt),
        in_specs=[
            pl.BlockSpec((TM, B, LRU_CW), lambda c, p, j: (tile(p, j), 0, c)),
            pl.BlockSpec((TM, B, LRU_CW), lambda c, p, j: (tile(p, j), 0, ncw + c)),
            pl.BlockSpec((2, B, LRU_CW), lambda c, p, j: (jnp.maximum(tile(p, j) * (TM // 2) - 1, 0), 0, c)),
            pl.BlockSpec((1, B, LRU_CW), lambda c, p, j: (jnp.minimum((tile(p, j) + 1) * TM, S - 1), 0, c)),
            pl.BlockSpec((4, LRU_CW), lambda c, p, j: (0, c)),
            pl.BlockSpec((1, LRU_CW), lambda c, p, j: (0, c)),
            pl.BlockSpec((1, 1, LRU_CW, LRU_CW), lambda c, p, j: (p, c, 0, 0)),
            pl.BlockSpec((1, 1, LRU_CW, LRU_CW), lambda c, p, j: (p, c, 0, 0)),
            vec(), vec(), vec(),
        ],
        out_specs=pl.BlockSpec((TM, B, LRU_CW), lambda c, p, j: (jnp.where(p == 0, nct - 1, tile(p, j)), 0, c)),
        out_shape=jax.ShapeDtypeStruct((S, B, BW), F32),
        scratch_shapes=[
            pltpu.VMEM((S, B, LRU_CW), F32),
            pltpu.VMEM((TM, B, LRU_CW), F32),
            pltpu.VMEM((TM, B, LRU_CW), F32),
            pltpu.VMEM((B, LRU_CW), F32),
        ],
        compiler_params=_cp(("arbitrary", "arbitrary", "arbitrary")),
        name="rglru",
    )(pa, pa, pa, pa, conv_w, conv_b, wa_bd, wx_bd, ba, bx, lam)


def _block_diag(w):
    per = LRU_CW // LRU_BS
    w = w.reshape(2, BW // LRU_CW, per, LRU_BS, LRU_BS)
    eye = jnp.eye(per, dtype=w.dtype)
    out = jnp.einsum('dcnio,nm->dcnimo', w, eye)
    return out.reshape(2, BW // LRU_CW, LRU_CW, LRU_CW)


HG_NLEV = 6
HG_ROWS = (HG_NLEV + 2) * HG_C


def _hgrn_constants():
    C = HG_C
    W = np.zeros((2, HG_NLEV + 2, C, C), np.float32)
    M = np.zeros((2, HG_NLEV + 1, C, C), np.float32)
    for lv in range(HG_NLEV):
        m = C >> (lv + 1)
        for t in range(C):
            blk = t // m
            if blk % 2 == 1:
                p = blk * m
                W[0, lv, t, p:t + 1] = 1.0
                W[1, lv, t, p:t] = 1.0
            else:
                p = (blk + 1) * m
                W[0, lv, t, t + 1:p] = 1.0
                W[1, lv, t, t:p] = 1.0
        for t in range(C):
            for s in range(C):
                if (t // m) % 2 == 1 and s // m == t // m - 1:
                    M[0, lv, t, s] = 1.0
                    M[1, lv, s, t] = 1.0
    for t in range(C):
        W[0, HG_NLEV, t, :t + 1] = 1.0
        W[1, HG_NLEV, t, t:] = 1.0
        W[0, HG_NLEV + 1, t, t + 1:] = 1.0
        W[1, HG_NLEV + 1, t, :t] = 1.0
        M[:, HG_NLEV, t, t] = 1.0
    W = W.reshape(2, HG_ROWS, C)
    return np.concatenate([W, W, W], axis=-1), M


def _hgrn_kernel(q_ref, f_ref, v_ref, og_ref, lbl_ref, ng_ref, w_ref, m_ref, y_ref, of_ref, st_ref, *, nt, layer):
    p = pl.program_id(1)
    j = pl.program_id(2)
    tile = _seq_tile(p, j, nt, CTX // TM)
    C = HG_C
    nchunk = TM // C

    @pl.when(j == 0)
    def _():
        st_ref[...] = jnp.zeros_like(st_ref)

    logits = lbl_ref[0]
    e = jnp.exp(logits - jnp.max(logits, axis=0, keepdims=True))
    soft = e / jnp.sum(e, axis=0, keepdims=True)
    lb = jnp.sum(soft[0:layer + 1], axis=0, keepdims=True) - soft[0:1]

    wmat = w_ref[0]
    masks = m_ref[0]
    row0 = pl.multiple_of(tile * TM, TM)

    def chunk(c0, backward):
        sl = slice(c0, c0 + C)
        f = lb + (1.0 - lb) * jax.nn.sigmoid(f_ref[0, sl, :])
        g = jnp.log(f)
        k = 1.0 - f
        qr = q_ref[0, sl, :]
        q = qr * jax.nn.sigmoid(qr)
        v = v_ref[0, sl, :]
        g1 = g.astype(BF16)
        r1 = g - g1.astype(F32)
        g2 = r1.astype(BF16)
        g3 = (r1 - g2.astype(F32)).astype(BF16)
        g_split = jnp.concatenate([g1, g2, g3], axis=0)
        ex = jnp.exp(jnp.dot(wmat, g_split, preferred_element_type=F32))
        outs = []
        for h in range(HG_H):
            hs = slice(h * HG_DK, (h + 1) * HG_DK)
            qh, kh = q[:, hs], k[:, hs]
            vh = v[:, hs].astype(BF16)
            eh = ex[:, hs].reshape(HG_NLEV + 2, C, HG_DK)
            qs = jnp.concatenate([qh[None] * eh[:HG_NLEV], qh[None]], axis=0).astype(BF16)
            ks = jnp.concatenate([kh[None] * eh[:HG_NLEV], kh[None]], axis=0).astype(BF16)
            att_l = jnp.einsum('lqd,lkd->lqk', qs, ks, preferred_element_type=F32)
            att = jnp.sum(jnp.where(masks > 0.5, att_l, 0.0), axis=0)
            st = st_ref[h]
            q_in = (qh * eh[HG_NLEV]).astype(BF16)
            o = jnp.dot(att.astype(BF16), vh, preferred_element_type=F32) \
                + lax.dot_general(q_in, st.astype(BF16), (((1,), (1,)), ((), ())), preferred_element_type=F32)
            dec = eh[HG_NLEV][0:1] if backward else eh[HG_NLEV][C - 1:C]
            k_st = (kh * eh[HG_NLEV + 1]).astype(BF16)
            st_ref[h] = dec * st + lax.dot_general(vh, k_st, (((0,), (0,)), ((), ())), preferred_element_type=F32)
            outs.append(o)
        return jnp.concatenate(outs, axis=-1)

    @pl.when(p == 0)
    def _():
        for ci in range(nchunk):
            o = chunk(ci * C, False)
            of_ref[pl.ds(row0 + ci * C, C), :] = o

    @pl.when(p == 1)
    def _():
        for ci in reversed(range(nchunk)):
            o = chunk(ci * C, True) + of_ref[pl.ds(row0 + ci * C, C), :]
            ys = []
            for h in range(HG_H):
                oh = o[:, h * HG_DK:(h + 1) * HG_DK]
                ys.append(oh * lax.rsqrt(jnp.mean(oh * oh, axis=-1, keepdims=True) + EPS))
            yn = jnp.concatenate(ys, axis=-1) * ng_ref[...]
            y_ref[0, ci * C:(ci + 1) * C, :] = yn * jax.nn.sigmoid(og_ref[0, ci * C:(ci + 1) * C, :])


def _hgrn2(pm, lb_logits, norm_g, layer):
    B, S, _ = pm.shape
    nt = S // TM
    tile = functools.partial(_seq_tile, nt=nt, nct=CTX // TM)
    w_np, m_np = _hgrn_constants()
    w_c = jnp.asarray(w_np, BF16)
    m_c = jnp.asarray(m_np, F32)
    col = lambda cb: pl.BlockSpec((1, TM, BW), lambda b, p, j: (b, tile(p, j), cb))
    return pl.pallas_call(
        functools.partial(_hgrn_kernel, nt=nt, layer=layer),
        grid=(B, 2, nt),
        in_specs=[
            col(0),
            pl.BlockSpec((1, TM, BW), lambda b, p, j: (b, tile(p, j), 1 + p)),
            col(3), col(4),
            pl.BlockSpec((1, DEPTH, BW), lambda b, p, j: (p, 0, 0)),
            pl.BlockSpec((1, BW), lambda b, p, j: (0, 0)),
            pl.BlockSpec((1, HG_ROWS, 3 * HG_C), lambda b, p, j: (p, 0, 0)),
            pl.BlockSpec((1, HG_NLEV + 1, HG_C, HG_C), lambda b, p, j: (p, 0, 0, 0)),
        ],
        out_specs=pl.BlockSpec((1, TM, BW), lambda b, p, j: (b, jnp.where(p == 0, CTX // TM - 1, tile(p, j)), 0)),
        out_shape=jax.ShapeDtypeStruct((B, S, BW), F32),
        scratch_shapes=[
            pltpu.VMEM((S, BW), F32),
            pltpu.VMEM((HG_H, HG_DK, HG_DK), F32),
        ],
        compiler_params=_cp(("arbitrary", "arbitrary", "arbitrary")),
        name="hgrn2",
    )(pm, pm, pm, pm, lb_logits, norm_g, w_c, m_c)


def _seg_mean_sq(x, seg):
    x2 = x * x
    hi = x2.astype(BF16)
    lo = (x2 - hi.astype(F32)).astype(BF16)
    s = jnp.dot(hi, seg, preferred_element_type=F32) + jnp.dot(lo, seg, preferred_element_type=F32)
    return s * (1.0 / NA_DH)


def _swap16(y):
    n = y.shape[-1]
    lane = lax.broadcasted_iota(jnp.int32, y.shape, 1)
    return jnp.where(lane % 32 < 16, pltpu.roll(y, n - 16, 1), pltpu.roll(y, 16, 1))


def _qkv_kernel(q_ref, k_ref, v_ref, qg_ref, kg_ref, seg_ref, cos_ref, sin_ref, qr_ref, qp_ref, kr_ref, vo_ref):
    seg = seg_ref[...]
    cos = cos_ref[...]
    sin = sin_ref[...]
    scale = NA_DH ** -0.5
    q = q_ref[0]
    qn = q * lax.rsqrt(_seg_mean_sq(q, seg) + EPS) * qg_ref[...]
    qp_ref[0] = (qn * scale).astype(BF16)
    qr_ref[0] = ((qn * cos + _swap16(qn) * sin) * scale).astype(BF16)
    k = k_ref[0]
    kn = k * lax.rsqrt(_seg_mean_sq(k, seg) + EPS) * kg_ref[...]
    kr_ref[0] = (kn * cos + _swap16(kn) * sin).astype(BF16)
    vo_ref[0] = v_ref[0].astype(BF16)


def _rope_tables(S):
    half = NA_DH // 2
    nf = half // 2
    t = np.arange(S - CTX)
    inv_freq = (ROPE_BASE ** (-np.arange(nf, dtype=np.float32) / nf)).astype(np.float32)
    ang_r = (t // GRID_W).astype(np.float32)[:, None] * inv_freq
    ang_c = (t % GRID_W).astype(np.float32)[:, None] * inv_freq
    ang = np.concatenate([ang_r, ang_r, ang_c, ang_c], axis=-1)
    sign = np.concatenate([-np.ones(nf), np.ones(nf), -np.ones(nf), np.ones(nf)]).astype(np.float32)
    cos = np.concatenate([np.ones((CTX, NA_DH), np.float32), np.cos(ang)], axis=0)
    sin = np.concatenate([np.zeros((CTX, NA_DH), np.float32), np.sin(ang) * sign], axis=0)
    return np.tile(cos, (1, NA_H)).astype(np.float32), np.tile(sin, (1, NA_H)).astype(np.float32)


def _qkv_prep(pm, q_g, k_g):
    B, S, _ = pm.shape
    nt = S // TM
    cos, sin = _rope_tables(S)
    seg = np.kron(np.eye(NA_H, dtype=np.float32), np.ones((NA_DH, NA_DH), np.float32))
    col = lambda cb: pl.BlockSpec((1, TM, BW), lambda b, j: (b, j, cb))
    full = lambda shape: pl.BlockSpec(shape, lambda b, j: (0,) * len(shape))
    out = pl.BlockSpec((1, TM, BW), lambda b, j: (b, j, 0))
    return pl.pallas_call(
        _qkv_kernel,
        grid=(B, nt),
        in_specs=[col(5), col(6), col(7), full((1, BW)), full((1, BW)), full((BW, BW)),
                  pl.BlockSpec((TM, BW), lambda b, j: (j, 0)), pl.BlockSpec((TM, BW), lambda b, j: (j, 0))],
        out_specs=[out, out, out, out],
        out_shape=[jax.ShapeDtypeStruct((B, S, BW), BF16)] * 4,
        compiler_params=_cp(("arbitrary", "arbitrary")),
        name="qkv_prep",
    )(pm, pm, pm, q_g, k_g, jnp.asarray(seg, BF16), jnp.asarray(cos), jnp.asarray(sin))


NA_GH = 4
NA_GW = NA_GH * NA_DH


def _head_mask(h, n_rows):
    lane = lax.broadcasted_iota(jnp.int32, (n_rows, NA_GW), 1)
    return lane // NA_DH == h


def _softmax_pv(s_parts, v_parts):
    m = s_parts[0].max(axis=-1, keepdims=True)
    for s in s_parts[1:]:
        m = jnp.maximum(m, s.max(axis=-1, keepdims=True))
    ps = [jnp.exp(s - m) for s in s_parts]
    den = ps[0].sum(axis=-1, keepdims=True)
    for pp in ps[1:]:
        den = den + pp.sum(axis=-1, keepdims=True)
    inv = 1.0 / den
    ps = [(pp * inv).astype(BF16) for pp in ps]
    acc = jnp.zeros((TM, NA_GW), F32)
    for h in range(NA_GH):
        for pp, vv in zip(ps, v_parts):
            vm = jnp.where(_head_mask(h, vv.shape[0]), vv, jnp.zeros_like(vv))
            acc = acc + jnp.dot(pp[h * TM:(h + 1) * TM], vm, preferred_element_type=F32)
    return acc.astype(BF16)


def _na_kernel(qr_ref, qp_ref, k0_ref, k1_ref, k2_ref, v0_ref, v1_ref, v2_ref, kc_ref, vc_ref, bias_ref, y_ref,
               *, with_ctx):
    step = pl.program_id(0)
    nkeys = NA_UR * GRID_W
    masks_q = [_head_mask(h, TM) for h in range(NA_GH)]

    def stack_heads(q):
        return jnp.concatenate([jnp.where(masks_q[h], q, jnp.zeros_like(q)) for h in range(NA_GH)], axis=0)

    nt_dims = (((1,), (1,)), ((), ()))

    def latent():
        for g in range(NA_H // NA_GH):
            gs = slice(g * NA_GW, (g + 1) * NA_GW)
            kw = jnp.concatenate([k0_ref[0, :, gs], k1_ref[0, :, gs], k2_ref[0, :, gs]], axis=0)
            vw = jnp.concatenate([v0_ref[0, :, gs], v1_ref[0, :, gs], v2_ref[0, :, gs]], axis=0)
            s_loc = lax.dot_general(stack_heads(qr_ref[0, :, gs]), kw, nt_dims, preferred_element_type=F32)
            s_loc = s_loc + bias_ref[0, g * NA_GH:(g + 1) * NA_GH].reshape(NA_GH * TM, nkeys)
            s_ctx = lax.dot_general(stack_heads(qp_ref[0, :, gs]), kc_ref[0, :, gs], nt_dims,
                                    preferred_element_type=F32)
            y_ref[0, :, gs] = _softmax_pv([s_loc, s_ctx], [vw, vc_ref[0, :, gs]])

    def context():
        for g in range(NA_H // NA_GH):
            gs = slice(g * NA_GW, (g + 1) * NA_GW)
            s_ctx = lax.dot_general(stack_heads(qp_ref[0, :, gs]), kc_ref[0, :, gs], nt_dims,
                                    preferred_element_type=F32)
            y_ref[0, :, gs] = _softmax_pv([s_ctx], [vc_ref[0, :, gs]])

    if with_ctx:
        pl.when(step == 0)(context)
        pl.when(step > 0)(latent)
    else:
        latent()


def _na_bias_tables(rpb):
    rows = 2048 // GRID_W
    col = np.arange(GRID_W)
    c_start = np.clip(col - WIN_C // 2, 0, GRID_W - WIN_C)
    col_ok = (col[None, :] >= c_start[:, None]) & (col[None, :] < c_start[:, None] + WIN_C)
    d_col = np.clip(col[None, :] - col[:, None], -(WIN_C - 1), WIN_C - 1) + (WIN_C - 1)
    hot = lambda idx, n: jnp.asarray(np.eye(n, dtype=np.float32)[idx])
    by_col = jnp.einsum('hrc,qkc->hrqk', rpb.astype(F32), hot(d_col, 2 * WIN_C - 1), precision=lax.Precision.HIGHEST)
    tabs = []
    for rb in (0, 1, rows // NA_RB - 1):
        r = rb * NA_RB + np.arange(NA_RB)
        u0 = int(np.clip(rb * NA_RB - NA_RB, 0, rows - NA_UR))
        kr = u0 + np.arange(NA_UR)
        kr0 = np.clip(r - WIN_R // 2, 0, rows - WIN_R)
        row_ok = (kr[None, :] >= kr0[:, None]) & (kr[None, :] < kr0[:, None] + WIN_R)
        d_row = np.clip(kr[None, :] - r[:, None] + (WIN_R - 1), 0, 2 * WIN_R - 2)
        ok = row_ok[:, None, :, None] & col_ok[None, :, None, :]
        bias = jnp.einsum('hrqk,ijr->hiqjk', by_col, hot(d_row, 2 * WIN_R - 1),
                          precision=lax.Precision.HIGHEST)
        bias = jnp.where(jnp.asarray(ok)[None], bias, MASK_VALUE)
        tabs.append(bias.reshape(NA_H, NA_RB * GRID_W, NA_UR * GRID_W))
    return jnp.stack(tabs)


def _na_key_base(step_latent):
    return jnp.clip(step_latent - 1, 0, 2048 // TM - NA_UR * GRID_W // TM)


def _attention(qr, qp, kr, vv, bias, with_ctx):
    B, S, _ = qr.shape
    nlat = (S - CTX) // TM
    off = 0 if with_ctx else 1

    def qtile(t, b):
        return (b, t + off, 0)

    def ktile(i):
        def f(t, b):
            return (b, 1 + _na_key_base(t + off - 1) + i, 0)
        return f

    def bias_idx(t, b):
        lat = t + off - 1
        return (jnp.where(lat <= 0, 0, jnp.where(lat == nlat - 1, 2, 1)), 0, 0, 0)

    blk = lambda f: pl.BlockSpec((1, TM, BW), f)
    return pl.pallas_call(
        functools.partial(_na_kernel, with_ctx=with_ctx),
        grid=(nlat + (1 if with_ctx else 0), B),
        in_specs=[blk(qtile), blk(qtile), blk(ktile(0)), blk(ktile(1)), blk(ktile(2)),
                  blk(ktile(0)), blk(ktile(1)), blk(ktile(2)),
                  blk(lambda t, b: (b, 0, 0)), blk(lambda t, b: (b, 0, 0)),
                  pl.BlockSpec((1, NA_H, TM, NA_UR * GRID_W), bias_idx)],
        out_specs=blk(qtile),
        out_shape=jax.ShapeDtypeStruct((B, S, BW), BF16),
        compiler_params=_cp(("arbitrary", "arbitrary")),
        name="attention",
    )(qr, qp, kr, kr, kr, vv, vv, vv, kr, vv, bias)


def _merge_kernel(h_ref, mod_ref, ya_ref, yb_ref, yc_ref, g0_ref, g1_ref, g2_ref, wb_ref, wo_ref, o_ref):
    m = mod_ref[0]
    acc = None
    for n, (y_ref, g_ref) in enumerate(((ya_ref, g0_ref), (yb_ref, g1_ref), (yc_ref, g2_ref))):
        y = y_ref[...].reshape(TM, BW).astype(BF16)
        term = jax.nn.sigmoid(g_ref[0]) * jnp.dot(y, wb_ref[n], preferred_element_type=F32)
        acc = term if acc is None else acc + term
    out = jnp.dot(acc.astype(BF16), wo_ref[...], preferred_element_type=F32)
    o_ref[0] = h_ref[0] + m[:, 2 * D:3 * D] * out


def _merge(h, mod_l, ya, yb, yc, pm, w_branch, w_out, first_tile):
    B, S, _ = h.shape
    nt = S // TM - first_tile
    tok = lambda w: pl.BlockSpec((1, TM, w), lambda b, j: (b, j + first_tile, 0))
    gate = lambda n: pl.BlockSpec((1, TM, D), lambda b, j: (b, j + first_tile, 4 + n))
    return pl.pallas_call(
        _merge_kernel,
        grid=(B, nt),
        in_specs=[
            tok(D),
            pl.BlockSpec((1, 1, 6 * D), lambda b, j: (_mod_row(b, j + first_tile), 0, 0)),
            pl.BlockSpec((TM, BW), lambda b, j: (j + first_tile, b)),
            tok(BW), tok(BW), gate(0), gate(1), gate(2),
            pl.BlockSpec((3, BW, D), lambda b, j: (0, 0, 0)),
            pl.BlockSpec((D, D), lambda b, j: (0, 0)),
        ],
        out_specs=pl.BlockSpec((1, TM, D), lambda b, j: (b, j, 0)),
        out_shape=jax.ShapeDtypeStruct((B, nt * TM, D), F32),
        compiler_params=_cp(("arbitrary", "arbitrary")),
        name="merge",
    )(h, mod_l, ya, yb, yc, pm, pm, pm, w_branch, w_out)


def _ffn_kernel(h_ref, mod_ref, g_ref, w1_ref, w2_ref, o_ref):
    x = h_ref[0]
    m = mod_ref[0]
    y = x * lax.rsqrt(jnp.mean(x * x, axis=-1, keepdims=True) + EPS) * g_ref[...]
    u = (y * (1.0 + m[:, 4 * D:5 * D]) + m[:, 3 * D:4 * D]).astype(BF16)
    a = jnp.maximum(jnp.dot(u, w1_ref[...], preferred_element_type=F32), 0.0)
    out = jnp.dot((a * a).astype(BF16), w2_ref[...], preferred_element_type=F32)
    o_ref[0] = x + m[:, 5 * D:6 * D] * out


def _ffn(h, mod_l, g, w1, w2, first_tile):
    B, n, _ = h.shape
    return pl.pallas_call(
        _ffn_kernel,
        grid=(B, n // TM),
        in_specs=[
            pl.BlockSpec((1, TM, D), lambda b, j: (b, j, 0)),
            pl.BlockSpec((1, 1, 6 * D), lambda b, j: (_mod_row(b, j + first_tile), 0, 0)),
            pl.BlockSpec((1, D), lambda b, j: (0, 0)),
            pl.BlockSpec((D, D_FF), lambda b, j: (0, 0), pipeline_mode=pl.Buffered(1)),
            pl.BlockSpec((D_FF, D), lambda b, j: (0, 0), pipeline_mode=pl.Buffered(1)),
        ],
        out_specs=pl.BlockSpec((1, TM, D), lambda b, j: (b, j, 0)),
        out_shape=jax.ShapeDtypeStruct((B, n, D), F32),
        compiler_params=_cp(("arbitrary", "arbitrary")),
        name="ffn",
    )(h, mod_l, g, w1, w2)


def kernel(x, c, ctx, c_ctx, ada_w, ada_b, norm1_g, norm2_g, w_in, conv_w, conv_b, lru_wa, lru_ba, lru_wx, lru_bx,
           lru_lambda, hg_lb_logits, hg_norm_g, na_q_norm_g, na_k_norm_g, na_rpb, w_branch, w_out, ffn_w1, ffn_w2):
    B, T, _ = x.shape
    S = CTX + T
    cond = jnp.zeros((16, D), F32).at[:B].set(c).at[8].set(c_ctx)
    mod = _modulation(cond, ada_w, ada_b)
    h = jnp.concatenate([ctx, x], axis=1)
    for l in range(DEPTH):
        last = l == DEPTH - 1
        mod_l = mod[l].reshape(16, 1, 6 * D)
        w_l = w_in[l].astype(BF16)
        pm, pa = _projection(h, mod_l, norm1_g[l][None], w_l[:, 2 * BW:], w_l[:, :2 * BW])

        ya = _rglru(pa.reshape(S, B, 2 * BW), conv_w[l], conv_b[l][None],
                    _block_diag(lru_wa[l]).astype(BF16), _block_diag(lru_wx[l]).astype(BF16),
                    lru_ba[l][:, None], lru_bx[l][:, None], lru_lambda[l][:, None])
        yb = _hgrn2(pm, hg_lb_logits, jnp.tile(hg_norm_g[l], HG_H)[None], l)
        qr, qp, kr, vv = _qkv_prep(pm, jnp.tile(na_q_norm_g[l], NA_H)[None], jnp.tile(na_k_norm_g[l], NA_H)[None])
        yc = _attention(qr, qp, kr, vv, _na_bias_tables(na_rpb[l]), with_ctx=not last)

        first = 1 if last else 0
        h1 = _merge(h, mod_l, ya.reshape(S, B * BW), yb, yc, pm, w_branch[l].astype(BF16), w_out[l].astype(BF16),
                    first)
        h = _ffn(h1, mod_l, norm2_g[l][None], ffn_w1[l].astype(BF16), ffn_w2[l].astype(BF16), first)
    return h
```

```python
import functools

import numpy as np
import jax
import jax.numpy as jnp
from jax import lax
from jax.experimental import pallas as pl
from jax.experimental.pallas import tpu as pltpu

F32 = jnp.float32
BF16 = jnp.bfloat16

D = 1024
DEPTH = 2
GRID_W = 64
CTX = 256
BW = 512
LRU_NB = 8
LRU_BS = BW // LRU_NB
LRU_C = 8.0
HG_H = 4
HG_DK = 128
HG_C = 64
NA_H = 8
NA_DH = 64
WIN_R = 8
WIN_C = 16
ROPE_BASE = 10000.0
MASK_VALUE = -1e30
D_FF = 4 * D
EPS = 1e-6
TM = 256
NA_RB = 4
NA_UR = 12
VMEM_LIMIT = 56 * 1024 * 1024


def _cp(sem):
    return pltpu.CompilerParams(dimension_semantics=sem, vmem_limit_bytes=VMEM_LIMIT)


def _mod_row(b, j):
    return jnp.where(j == 0, 8, b)


def _mod_kernel(c_ref, w_ref, b_ref, o_ref):
    c = c_ref[...]
    s = c * jax.nn.sigmoid(c)
    o_ref[0] = jnp.dot(s, w_ref[0], preferred_element_type=F32) + b_ref[0]


def _modulation(cond, ada_w, ada_b):
    tn = 1536
    return pl.pallas_call(
        _mod_kernel,
        grid=(DEPTH, 6 * D // tn),
        in_specs=[
            pl.BlockSpec((16, D), lambda l, n: (0, 0)),
            pl.BlockSpec((1, D, tn), lambda l, n: (l, 0, n)),
            pl.BlockSpec((1, 1, tn), lambda l, n: (l, 0, n)),
        ],
        out_specs=pl.BlockSpec((1, 16, tn), lambda l, n: (l, 0, n)),
        out_shape=jax.ShapeDtypeStruct((DEPTH, 16, 6 * D), F32),
        compiler_params=_cp(("arbitrary", "arbitrary")),
        name="modulation",
    )(cond, ada_w, ada_b.reshape(DEPTH, 1, 6 * D))


def _proj_kernel(x_ref, mod_ref, g_ref, w_ref, p_ref, pa_ref):
    x = x_ref[0]
    m = mod_ref[0]
    y = x * lax.rsqrt(jnp.mean(x * x, axis=-1, keepdims=True) + EPS) * g_ref[...]
    u = (y * (1.0 + m[:, D:2 * D]) + m[:, 0:D]).astype(BF16)
    p_ref[0] = jnp.dot(u, w_ref[:, 2 * BW:], preferred_element_type=F32).astype(p_ref.dtype)
    pa_ref[...] = jnp.dot(u, w_ref[:, :2 * BW], preferred_element_type=F32)


def _projection(h, mod_l, g, w):
    B, S, _ = h.shape
    nt = S // TM
    nw = w.shape[1]
    nm = nw - 2 * BW
    return pl.pallas_call(
        _proj_kernel,
        grid=(B, nt),
        in_specs=[
            pl.BlockSpec((1, TM, D), lambda b, j: (b, j, 0)),
            pl.BlockSpec((1, 1, 6 * D), lambda b, j: (_mod_row(b, j), 0, 0)),
            pl.BlockSpec((1, D), lambda b, j: (0, 0)),
            pl.BlockSpec((D, nw), lambda b, j: (0, 0), pipeline_mode=pl.Buffered(1)),
        ],
        out_specs=[
            pl.BlockSpec((1, TM, nm), lambda b, j: (b, j, 0)),
            pl.BlockSpec((TM, 2 * BW), lambda b, j: (j, b)),
        ],
        out_shape=[
            jax.ShapeDtypeStruct((B, S, nm), BF16),
            jax.ShapeDtypeStruct((S, B * 2 * BW), F32),
        ],
        compiler_params=_cp(("arbitrary", "arbitrary")),
        name="projection",
    )(h, mod_l, g, w)


LRU_CW = 256
LRU_TM = 128


def _seq_tile(p, j, nt, nct):
    return jnp.where(p == 0, j, jnp.where(j < nct, nct - 1 - j, nt - 1 - (j - nct)))


def _softplus(x):
    return jnp.maximum(x, 0.0) + jnp.log1p(jnp.exp(-jnp.abs(x)))


def _gelu_tanh(x):
    return 0.5 * x * (1.0 + jnp.tanh(np.sqrt(2.0 / np.pi) * (x + 0.044715 * (x * x * x))))


def _lru_kernel(x_ref, gate_ref, prev_ref, next_ref, cw_ref, cb_ref, wa_ref, wx_ref, ba_ref, bx_ref, lam_ref,
                y_ref, hf_ref, a_ref, b_ref, carry_ref, *, nt, nct):
    TM = LRU_TM
    p = pl.program_id(1)
    j = pl.program_id(2)
    tile = _seq_tile(p, j, nt, nct)
    nb = x_ref.shape[1]

    @pl.when(j == 0)
    def _():
        carry_ref[...] = jnp.zeros_like(carry_ref)

    x = x_ref[...]
    has_prev = jnp.logical_and(tile != 0, tile != nct)
    has_next = jnp.logical_and(tile != nct - 1, tile != nt - 1)
    prev = jnp.where(has_prev, prev_ref[...], 0.0)
    nxt = jnp.where(has_next, next_ref[...], 0.0)
    xe = jnp.concatenate([prev, x, nxt], axis=0)
    cw = cw_ref[...]
    xc = cb_ref[...][None] + xe[0:TM] * cw[0:1][None] + xe[1:TM + 1] * cw[1:2][None] \
        + xe[2:TM + 2] * cw[2:3][None] + xe[3:TM + 3] * cw[3:4][None]
    xc2 = xc.reshape(TM * nb, LRU_CW)
    xb = xc2.astype(BF16)
    r = jax.nn.sigmoid(jnp.dot(xb, wa_ref[0, 0], preferred_element_type=F32) + ba_ref[0])
    i = jax.nn.sigmoid(jnp.dot(xb, wx_ref[0, 0], preferred_element_type=F32) + bx_ref[0])
    log_a = (-LRU_C) * r * _softplus(-lam_ref[0])
    a = jnp.exp(log_a)
    bc = jnp.sqrt(1.0 - a * a) * (i * xc2)
    a_ref[...] = a.reshape(TM, nb, LRU_CW)
    b_ref[...] = bc.reshape(TM, nb, LRU_CW)

    def step(t, h):
        h = a_ref[t] * h + b_ref[t]
        b_ref[t] = h
        return h

    @pl.when(p == 0)
    def _():
        carry_ref[...] = lax.fori_loop(0, TM, step, carry_ref[...], unroll=8)
        hf_ref[pl.ds(pl.multiple_of(tile * TM, TM), TM)] = b_ref[...]

    @pl.when(p == 1)
    def _():
        carry_ref[...] = lax.fori_loop(0, TM, lambda s, h: step(TM - 1 - s, h), carry_ref[...], unroll=8)
        hf = hf_ref[pl.ds(pl.multiple_of(tile * TM, TM), TM)]
        y_ref[...] = (hf + b_ref[...]) * _gelu_tanh(gate_ref[...])


def _rglru(pa, conv_w, conv_b, wa_bd, wx_bd, ba, bx, lam):
    S, B, _ = pa.shape
    TM = LRU_TM
    nt = S // TM
    nct = CTX // TM
    ncw = BW // LRU_CW
    tile = functools.partial(_seq_tile, nt=nt, nct=nct)
    vec = lambda: pl.BlockSpec((1, 1, LRU_CW), lambda c, p, j: (p, 0, c))
    return pl.pallas_call(
        functools.partial(_lru_kernel, nt=nt, nct=nct),
        grid=(ncw, 2, nt),
        in_specs=[
            pl.BlockSpec((TM, B, LRU_CW), lambda c, p, j: (tile(p, j), 0, c)),
            pl.BlockSpec((TM, B, LRU_CW), lambda c, p, j: (tile(p, j), 0, ncw + c)),
            pl.BlockSpec((2, B, LRU_CW), lambda c, p, j: (jnp.maximum(tile(p, j) * (TM // 2) - 1, 0), 0, c)),
            pl.BlockSpec((1, B, LRU_CW), lambda c, p, j: (jnp.minimum((tile(p, j) + 1) * TM, S - 1), 0, c)),
            pl.BlockSpec((4, LRU_CW), lambda c, p, j: (0, c)),
            pl.BlockSpec((1, LRU_CW), lambda c, p, j: (0, c)),
            pl.BlockSpec((1, 1, LRU_CW, LRU_CW), lambda c, p, j: (p, c, 0, 0)),
            pl.BlockSpec((1, 1, LRU_CW, LRU_CW), lambda c, p, j: (p, c, 0, 0)),
            vec(), vec(), vec(),
        ],
        out_specs=pl.BlockSpec((TM, B, LRU_CW), lambda c, p, j: (jnp.where(p == 0, nct - 1, tile(p, j)), 0, c)),
        out_shape=jax.ShapeDtypeStruct((S, B, BW), F32),
        scratch_shapes=[
            pltpu.VMEM((S, B, LRU_CW), F32),
            pltpu.VMEM((TM, B, LRU_CW), F32),
            pltpu.VMEM((TM, B, LRU_CW), F32),
            pltpu.VMEM((B, LRU_CW), F32),
        ],
        compiler_params=_cp(("arbitrary", "arbitrary", "arbitrary")),
        name="rglru",
    )(pa, pa, pa, pa, conv_w, conv_b, wa_bd, wx_bd, ba, bx, lam)


def _block_diag(w):
    per = LRU_CW // LRU_BS
    w = w.reshape(2, BW // LRU_CW, per, LRU_BS, LRU_BS)
    eye = jnp.eye(per, dtype=w.dtype)
    out = jnp.einsum('dcnio,nm->dcnimo', w, eye)
    return out.reshape(2, BW // LRU_CW, LRU_CW, LRU_CW)


HG_NLEV = 6
HG_ROWS = (HG_NLEV + 2) * HG_C
LOG2E = 1.4426950408889634


def _hgrn_constants():
    C = HG_C
    W = np.zeros((2, HG_NLEV + 2, C, C), np.float32)
    M = np.zeros((2, HG_NLEV + 1, C, C), np.float32)
    for lv in range(HG_NLEV):
        m = C >> (lv + 1)
        for t in range(C):
            blk = t // m
            if blk % 2 == 1:
                p = blk * m
                W[0, lv, t, p:t + 1] = 1.0
                W[1, lv, t, p:t] = 1.0
            else:
                p = (blk + 1) * m
                W[0, lv, t, t + 1:p] = 1.0
                W[1, lv, t, t:p] = 1.0
        for t in range(C):
            for s in range(C):
                if (t // m) % 2 == 1 and s // m == t // m - 1:
                    M[0, lv, t, s] = 1.0
                    M[1, lv, s, t] = 1.0
    for t in range(C):
        W[0, HG_NLEV, t, :t + 1] = 1.0
        W[1, HG_NLEV, t, t:] = 1.0
        W[0, HG_NLEV + 1, t, t + 1:] = 1.0
        W[1, HG_NLEV + 1, t, :t] = 1.0
        M[:, HG_NLEV, t, t] = 1.0
    W = W.reshape(2, HG_ROWS, C)
    lev = np.full((2, C, C), -1, np.int32)
    for lv in range(HG_NLEV + 1):
        lev[M[:, lv] > 0.5] = lv
    return np.concatenate([W, W], axis=-1), lev


def _hgrn_kernel(q_ref, f_ref, v_ref, og_ref, lbl_ref, ng_ref, w_ref, m_ref, y_ref, of_ref, st_ref, *, nt, layer):
    p = pl.program_id(1)
    j = pl.program_id(2)
    tile = _seq_tile(p, j, nt, CTX // TM)
    C = HG_C
    nchunk = TM // C

    @pl.when(j == 0)
    def _():
        st_ref[...] = jnp.zeros_like(st_ref)

    logits = lbl_ref[0]
    e = jnp.exp(logits - jnp.max(logits, axis=0, keepdims=True))
    soft = e / jnp.sum(e, axis=0, keepdims=True)
    lb = jnp.sum(soft[0:layer + 1], axis=0, keepdims=True) - soft[0:1]

    wmat = w_ref[0]
    lev = m_ref[0]
    row0 = pl.multiple_of(tile * TM, TM)
    inter = HG_NLEV * C

    def direction(backward):
        f = lb + (1.0 - lb) * jax.nn.sigmoid(f_ref[0].astype(F32))
        g = jnp.log(f) * LOG2E
        k = (1.0 - f).astype(BF16)
        qr = q_ref[0].astype(F32)
        q = (qr * jax.nn.sigmoid(qr)).astype(BF16)
        v = v_ref[0]
        g1 = g.astype(BF16)
        g2 = (g - g1.astype(F32)).astype(BF16)
        g_split = jnp.concatenate(
            [jnp.concatenate([g1[c * C:(c + 1) * C], g2[c * C:(c + 1) * C]], axis=0) for c in range(nchunk)], axis=1)
        ex32 = jnp.exp2(jnp.dot(wmat, g_split, preferred_element_type=F32))
        ex = ex32.astype(BF16)
        dec_row = inter if backward else inter + C - 1
        dec_all = ex32[dec_row:dec_row + 1]
        intra, q_in, upd = {}, {}, {}
        for c in range(nchunk):
            rows = slice(c * C, (c + 1) * C)
            for h in range(HG_H):
                hs = slice(h * HG_DK, (h + 1) * HG_DK)
                qh, kh, vh = q[rows, hs], k[rows, hs], v[rows, hs]
                eh = ex[:, c * BW + h * HG_DK:c * BW + (h + 1) * HG_DK].reshape(HG_NLEV + 2, C, HG_DK)
                qs = jnp.concatenate([qh[None] * eh[:HG_NLEV], qh[None]], axis=0)
                ks = jnp.concatenate([kh[None] * eh[:HG_NLEV], kh[None]], axis=0)
                att_l = jnp.einsum('lqd,lkd->lqk', qs, ks, preferred_element_type=F32)
                att = jnp.zeros((C, C), F32)
                for lv in range(HG_NLEV + 1):
                    att = jnp.where(lev == lv, att_l[lv], att)
                intra[c, h] = jnp.dot(att.astype(BF16), vh, preferred_element_type=F32)
                q_in[c, h] = qh * eh[HG_NLEV]
                upd[c, h] = lax.dot_general(vh, kh * eh[HG_NLEV + 1], (((0,), (0,)), ((), ())),
                                            preferred_element_type=F32)
        st = [st_ref[h] for h in range(HG_H)]
        outs = [None] * nchunk
        for c in (reversed(range(nchunk)) if backward else range(nchunk)):
            row = []
            for h in range(HG_H):
                row.append(intra[c, h] + lax.dot_general(q_in[c, h], st[h].astype(BF16), (((1,), (1,)), ((), ())),
                                                         preferred_element_type=F32))
                st[h] = dec_all[:, c * BW + h * HG_DK:c * BW + (h + 1) * HG_DK] * st[h] + upd[c, h]
            outs[c] = jnp.concatenate(row, axis=-1)
        for h in range(HG_H):
            st_ref[h] = st[h]
        return jnp.concatenate(outs, axis=0)

    @pl.when(p == 0)
    def _():
        of_ref[pl.ds(row0, TM), :] = direction(False)

    @pl.when(p == 1)
    def _():
        o = direction(True) + of_ref[pl.ds(row0, TM), :]
        ys = []
        for h in range(HG_H):
            oh = o[:, h * HG_DK:(h + 1) * HG_DK]
            ys.append(oh * lax.rsqrt(jnp.mean(oh * oh, axis=-1, keepdims=True) + EPS))
        yn = jnp.concatenate(ys, axis=-1) * ng_ref[...]
        y_ref[0] = (yn * jax.nn.sigmoid(og_ref[0].astype(F32))).astype(y_ref.dtype)


def _hgrn2(pm, lb_logits, norm_g, layer):
    B, S, _ = pm.shape
    nt = S // TM
    tile = functools.partial(_seq_tile, nt=nt, nct=CTX // TM)
    w_np, m_np = _hgrn_constants()
    w_c = jnp.asarray(w_np, BF16)
    m_c = jnp.asarray(m_np, jnp.int32)
    col = lambda cb: pl.BlockSpec((1, TM, BW), lambda b, p, j: (b, tile(p, j), cb))
    return pl.pallas_call(
        functools.partial(_hgrn_kernel, nt=nt, layer=layer),
        grid=(B, 2, nt),
        in_specs=[
            col(0),
            pl.BlockSpec((1, TM, BW), lambda b, p, j: (b, tile(p, j), 1 + p)),
            col(3), col(4),
            pl.BlockSpec((1, DEPTH, BW), lambda b, p, j: (p, 0, 0)),
            pl.BlockSpec((1, BW), lambda b, p, j: (0, 0)),
            pl.BlockSpec((1, HG_ROWS, 2 * HG_C), lambda b, p, j: (p, 0, 0)),
            pl.BlockSpec((1, HG_C, HG_C), lambda b, p, j: (p, 0, 0)),
        ],
        out_specs=pl.BlockSpec((1, TM, BW), lambda b, p, j: (b, jnp.where(p == 0, CTX // TM - 1, tile(p, j)), 0)),
        out_shape=jax.ShapeDtypeStruct((B, S, BW), BF16),
        scratch_shapes=[
            pltpu.VMEM((S, BW), F32),
            pltpu.VMEM((HG_H, HG_DK, HG_DK), F32),
        ],
        compiler_params=_cp(("arbitrary", "arbitrary", "arbitrary")),
        name="hgrn2",
    )(pm, pm, pm, pm, lb_logits, norm_g, w_c, m_c)


def _seg_mean_sq(x, seg):
    s = jnp.dot((x * x).astype(BF16), seg, preferred_element_type=F32)
    return s * (1.0 / NA_DH)


def _swap16(y):
    n = y.shape[-1]
    lane = lax.broadcasted_iota(jnp.int32, y.shape, 1)
    return jnp.where(lane % 32 < 16, pltpu.roll(y, n - 16, 1), pltpu.roll(y, 16, 1))


def _qkv_kernel(q_ref, k_ref, qg_ref, kg_ref, seg_ref, cos_ref, sin_ref, qr_ref, qp_ref, kr_ref):
    seg = seg_ref[...]
    cos = cos_ref[...]
    sin = sin_ref[...]
    scale = NA_DH ** -0.5
    q = q_ref[0].astype(F32)
    qn = q * lax.rsqrt(_seg_mean_sq(q, seg) + EPS) * qg_ref[...]
    qp_ref[0] = (qn * scale).astype(BF16)
    qr_ref[0] = ((qn * cos + _swap16(qn) * sin) * scale).astype(BF16)
    k = k_ref[0].astype(F32)
    kn = k * lax.rsqrt(_seg_mean_sq(k, seg) + EPS) * kg_ref[...]
    kr_ref[0] = (kn * cos + _swap16(kn) * sin).astype(BF16)


def _rope_tables(S):
    half = NA_DH // 2
    nf = half // 2
    t = np.arange(S - CTX)
    inv_freq = (ROPE_BASE ** (-np.arange(nf, dtype=np.float32) / nf)).astype(np.float32)
    ang_r = (t // GRID_W).astype(np.float32)[:, None] * inv_freq
    ang_c = (t % GRID_W).astype(np.float32)[:, None] * inv_freq
    ang = np.concatenate([ang_r, ang_r, ang_c, ang_c], axis=-1)
    sign = np.concatenate([-np.ones(nf), np.ones(nf), -np.ones(nf), np.ones(nf)]).astype(np.float32)
    cos = np.concatenate([np.ones((CTX, NA_DH), np.float32), np.cos(ang)], axis=0)
    sin = np.concatenate([np.zeros((CTX, NA_DH), np.float32), np.sin(ang) * sign], axis=0)
    return np.tile(cos, (1, NA_H)).astype(np.float32), np.tile(sin, (1, NA_H)).astype(np.float32)


def _qkv_prep(pm, q_g, k_g):
    B, S, _ = pm.shape
    nt = S // TM
    cos, sin = _rope_tables(S)
    seg = np.kron(np.eye(NA_H, dtype=np.float32), np.ones((NA_DH, NA_DH), np.float32))
    col = lambda cb: pl.BlockSpec((1, TM, BW), lambda b, j: (b, j, cb))
    full = lambda shape: pl.BlockSpec(shape, lambda b, j: (0,) * len(shape))
    out = pl.BlockSpec((1, TM, BW), lambda b, j: (b, j, 0))
    return pl.pallas_call(
        _qkv_kernel,
        grid=(B, nt),
        in_specs=[col(5), col(6), full((1, BW)), full((1, BW)), full((BW, BW)),
                  pl.BlockSpec((TM, BW), lambda b, j: (j, 0)), pl.BlockSpec((TM, BW), lambda b, j: (j, 0))],
        out_specs=[out, out, out],
        out_shape=[jax.ShapeDtypeStruct((B, S, BW), BF16)] * 3,
        compiler_params=_cp(("arbitrary", "arbitrary")),
        name="qkv_prep",
    )(pm, pm, q_g, k_g, jnp.asarray(seg, BF16), jnp.asarray(cos), jnp.asarray(sin))


NA_GH = 4
NA_GW = NA_GH * NA_DH


def _head_mask(h, n_rows):
    lane = lax.broadcasted_iota(jnp.int32, (n_rows, NA_GW), 1)
    return lane // NA_DH == h


def _softmax_pv(s_parts, v_parts):
    m = s_parts[0].max(axis=-1, keepdims=True)
    for s in s_parts[1:]:
        m = jnp.maximum(m, s.max(axis=-1, keepdims=True))
    ps = [jnp.exp(s - m) for s in s_parts]
    den = ps[0].sum(axis=-1, keepdims=True)
    for pp in ps[1:]:
        den = den + pp.sum(axis=-1, keepdims=True)
    inv = 1.0 / den
    ps = [(pp * inv).astype(BF16) for pp in ps]
    acc = jnp.zeros((TM, NA_GW), F32)
    for h in range(NA_GH):
        for pp, vv in zip(ps, v_parts):
            vm = jnp.where(_head_mask(h, vv.shape[0]), vv, jnp.zeros_like(vv))
            acc = acc + jnp.dot(pp[h * TM:(h + 1) * TM], vm, preferred_element_type=F32)
    return acc.astype(BF16)


def _na_kernel(qr_ref, qp_ref, k0_ref, k1_ref, k2_ref, v0_ref, v1_ref, v2_ref, kc_ref, vc_ref, bias_ref, y_ref,
               *, with_ctx):
    step = pl.program_id(0)
    nkeys = NA_UR * GRID_W
    masks_q = [_head_mask(h, TM) for h in range(NA_GH)]

    def stack_heads(q):
        return jnp.concatenate([jnp.where(masks_q[h], q, jnp.zeros_like(q)) for h in range(NA_GH)], axis=0)

    nt_dims = (((1,), (1,)), ((), ()))

    def latent():
        for g in range(NA_H // NA_GH):
            gs = slice(g * NA_GW, (g + 1) * NA_GW)
            kw = jnp.concatenate([k0_ref[0, :, gs], k1_ref[0, :, gs], k2_ref[0, :, gs]], axis=0)
            vw = jnp.concatenate([v0_ref[0, :, gs], v1_ref[0, :, gs], v2_ref[0, :, gs]], axis=0)
            s_loc = lax.dot_general(stack_heads(qr_ref[0, :, gs]), kw, nt_dims, preferred_element_type=F32)
            s_loc = s_loc + bias_ref[0, g * NA_GH:(g + 1) * NA_GH].reshape(NA_GH * TM, nkeys)
            s_ctx = lax.dot_general(stack_heads(qp_ref[0, :, gs]), kc_ref[0, :, gs], nt_dims,
                                    preferred_element_type=F32)
            y_ref[0, :, gs] = _softmax_pv([s_loc, s_ctx], [vw, vc_ref[0, :, gs]])

    def context():
        for g in range(NA_H // NA_GH):
            gs = slice(g * NA_GW, (g + 1) * NA_GW)
            s_ctx = lax.dot_general(stack_heads(qp_ref[0, :, gs]), kc_ref[0, :, gs], nt_dims,
                                    preferred_element_type=F32)
            y_ref[0, :, gs] = _softmax_pv([s_ctx], [vc_ref[0, :, gs]])

    if with_ctx:
        pl.when(step == 0)(context)
        pl.when(step > 0)(latent)
    else:
        latent()


def _na_bias_tables(rpb):
    rows = 2048 // GRID_W
    col = np.arange(GRID_W)
    c_start = np.clip(col - WIN_C // 2, 0, GRID_W - WIN_C)
    col_ok = (col[None, :] >= c_start[:, None]) & (col[None, :] < c_start[:, None] + WIN_C)
    d_col = np.clip(col[None, :] - col[:, None], -(WIN_C - 1), WIN_C - 1) + (WIN_C - 1)
    hot = lambda idx, n: jnp.asarray(np.eye(n, dtype=np.float32)[idx])
    by_col = jnp.einsum('hrc,qkc->hrqk', rpb.astype(F32), hot(d_col, 2 * WIN_C - 1), precision=lax.Precision.HIGHEST)
    tabs = []
    for rb in (0, 1, rows // NA_RB - 1):
        r = rb * NA_RB + np.arange(NA_RB)
        u0 = int(np.clip(rb * NA_RB - NA_RB, 0, rows - NA_UR))
        kr = u0 + np.arange(NA_UR)
        kr0 = np.clip(r - WIN_R // 2, 0, rows - WIN_R)
        row_ok = (kr[None, :] >= kr0[:, None]) & (kr[None, :] < kr0[:, None] + WIN_R)
        d_row = np.clip(kr[None, :] - r[:, None] + (WIN_R - 1), 0, 2 * WIN_R - 2)
        ok = row_ok[:, None, :, None] & col_ok[None, :, None, :]
        bias = jnp.einsum('hrqk,ijr->hiqjk', by_col, hot(d_row, 2 * WIN_R - 1),
                          precision=lax.Precision.HIGHEST)
        bias = jnp.where(jnp.asarray(ok)[None], bias, MASK_VALUE)
        tabs.append(bias.reshape(NA_H, NA_RB * GRID_W, NA_UR * GRID_W))
    return jnp.stack(tabs)


def _na_key_base(step_latent):
    return jnp.clip(step_latent - 1, 0, 2048 // TM - NA_UR * GRID_W // TM)


NA_VCOL = 7


def _attention(qr, qp, kr, pm, bias, with_ctx):
    B, S, _ = qr.shape
    nlat = (S - CTX) // TM
    off = 0 if with_ctx else 1

    def qtile(t, b):
        return (b, t + off, 0)

    def ktile(i, cb=0):
        def f(t, b):
            return (b, 1 + _na_key_base(t + off - 1) + i, cb)
        return f

    def bias_idx(t, b):
        lat = t + off - 1
        return (jnp.where(lat <= 0, 0, jnp.where(lat == nlat - 1, 2, 1)), 0, 0, 0)

    blk = lambda f: pl.BlockSpec((1, TM, BW), f)
    return pl.pallas_call(
        functools.partial(_na_kernel, with_ctx=with_ctx),
        grid=(nlat + (1 if with_ctx else 0), B),
        in_specs=[blk(qtile), blk(qtile), blk(ktile(0)), blk(ktile(1)), blk(ktile(2)),
                  blk(ktile(0, NA_VCOL)), blk(ktile(1, NA_VCOL)), blk(ktile(2, NA_VCOL)),
                  blk(lambda t, b: (b, 0, 0)), blk(lambda t, b: (b, 0, NA_VCOL)),
                  pl.BlockSpec((1, NA_H, TM, NA_UR * GRID_W), bias_idx)],
        out_specs=blk(lambda t, b: (b, t, 0)),
        out_shape=jax.ShapeDtypeStruct((B, S - off * TM, BW), BF16),
        compiler_params=_cp(("arbitrary", "arbitrary")),
        name="attention",
    )(qr, qp, kr, kr, kr, pm, pm, pm, kr, pm, bias)


def _merge_ffn_kernel(h_ref, mod_ref, ya_ref, yb_ref, yc_ref, g0_ref, g1_ref, g2_ref, wb_ref, wo_ref, ng_ref,
                      w1_ref, w2_ref, o_ref):
    m = mod_ref[0]
    acc = None
    for n, (y_ref, g_ref) in enumerate(((ya_ref, g0_ref), (yb_ref, g1_ref), (yc_ref, g2_ref))):
        y = y_ref[...].reshape(TM, BW).astype(BF16)
        term = jax.nn.sigmoid(g_ref[0].astype(F32)) * jnp.dot(y, wb_ref[n], preferred_element_type=F32)
        acc = term if acc is None else acc + term
    mixed = jnp.dot(acc.astype(BF16), wo_ref[...], preferred_element_type=F32)
    x = h_ref[0] + m[:, 2 * D:3 * D] * mixed
    y = x * lax.rsqrt(jnp.mean(x * x, axis=-1, keepdims=True) + EPS) * ng_ref[...]
    u = (y * (1.0 + m[:, 4 * D:5 * D]) + m[:, 3 * D:4 * D]).astype(BF16)
    a = jnp.maximum(jnp.dot(u, w1_ref[...], preferred_element_type=F32), 0.0)
    out = jnp.dot((a * a).astype(BF16), w2_ref[...], preferred_element_type=F32)
    o_ref[0] = x + m[:, 5 * D:6 * D] * out


def _merge_ffn(h, mod_l, ya, yb, yc, pm, w_branch, w_out, norm_g, w1, w2, first_tile):
    B, S, _ = h.shape
    nt = S // TM - first_tile
    tok = lambda w: pl.BlockSpec((1, TM, w), lambda b, j: (b, j + first_tile, 0))
    gate = lambda n: pl.BlockSpec((1, TM, D), lambda b, j: (b, j + first_tile, 4 + n))
    const = lambda shape: pl.BlockSpec(shape, lambda b, j: (0,) * len(shape), pipeline_mode=pl.Buffered(1))
    return pl.pallas_call(
        _merge_ffn_kernel,
        grid=(B, nt),
        in_specs=[
            tok(D),
            pl.BlockSpec((1, 1, 6 * D), lambda b, j: (_mod_row(b, j + first_tile), 0, 0)),
            pl.BlockSpec((TM, BW), lambda b, j: (j + first_tile, b)),
            tok(BW),
            pl.BlockSpec((1, TM, BW), lambda b, j: (b, j, 0)),
            gate(0), gate(1), gate(2),
            const((3, BW, D)), const((D, D)), const((1, D)), const((D, D_FF)), const((D_FF, D)),
        ],
        out_specs=pl.BlockSpec((1, TM, D), lambda b, j: (b, j, 0)),
        out_shape=jax.ShapeDtypeStruct((B, nt * TM, D), F32),
        compiler_params=_cp(("arbitrary", "arbitrary")),
        name="merge_ffn",
    )(h, mod_l, ya, yb, yc, pm, pm, pm, w_branch, w_out, norm_g, w1, w2)


def kernel(x, c, ctx, c_ctx, ada_w, ada_b, norm1_g, norm2_g, w_in, conv_w, conv_b, lru_wa, lru_ba, lru_wx, lru_bx,
           lru_lambda, hg_lb_logits, hg_norm_g, na_q_norm_g, na_k_norm_g, na_rpb, w_branch, w_out, ffn_w1, ffn_w2):
    B, T, _ = x.shape
    S = CTX + T
    cond = jnp.zeros((16, D), F32).at[:B].set(c).at[8].set(c_ctx)
    mod = _modulation(cond, ada_w, ada_b)
    h = jnp.concatenate([ctx, x], axis=1)
    for l in range(DEPTH):
        last = l == DEPTH - 1
        mod_l = mod[l].reshape(16, 1, 6 * D)
        pm, pa = _projection(h, mod_l, norm1_g[l][None], w_in[l].astype(BF16))

        ya = _rglru(pa.reshape(S, B, 2 * BW), conv_w[l], conv_b[l][None],
                    _block_diag(lru_wa[l]).astype(BF16), _block_diag(lru_wx[l]).astype(BF16),
                    lru_ba[l][:, None], lru_bx[l][:, None], lru_lambda[l][:, None])
        yb = _hgrn2(pm, hg_lb_logits, jnp.tile(hg_norm_g[l], HG_H)[None], l)
        qr, qp, kr = _qkv_prep(pm, jnp.tile(na_q_norm_g[l], NA_H)[None], jnp.tile(na_k_norm_g[l], NA_H)[None])
        yc = _attention(qr, qp, kr, pm, _na_bias_tables(na_rpb[l]), with_ctx=not last)

        h = _merge_ffn(h, mod_l, ya.reshape(S, B * BW), yb, yc, pm, w_branch[l].astype(BF16), w_out[l].astype(BF16),
                       norm2_g[l][None], ffn_w1[l].astype(BF16), ffn_w2[l].astype(BF16), 1 if last else 0)
    return h
```

```python
import functools

import numpy as np
import jax
import jax.numpy as jnp
from jax import lax
from jax.experimental import pallas as pl
from jax.experimental.pallas import tpu as pltpu

F32 = jnp.float32
BF16 = jnp.bfloat16

D = 1024
DEPTH = 2
GRID_W = 64
CTX = 256
BW = 512
LRU_NB = 8
LRU_BS = BW // LRU_NB
LRU_C = 8.0
HG_H = 4
HG_DK = 128
HG_C = 64
NA_H = 8
NA_DH = 64
WIN_R = 8
WIN_C = 16
ROPE_BASE = 10000.0
MASK_VALUE = -1e30
D_FF = 4 * D
EPS = 1e-6
TM = 256
NA_RB = 4
NA_UR = 12
VMEM_LIMIT = 56 * 1024 * 1024
LOG2E = 1.4426950408889634


def _cp(sem):
    return pltpu.CompilerParams(dimension_semantics=sem, vmem_limit_bytes=VMEM_LIMIT)


def _mod_row(b, j):
    return jnp.where(j == 0, 8, b)


def _mod_kernel(c_ref, w_ref, b_ref, o_ref):
    c = c_ref[...]
    s = c * jax.nn.sigmoid(c)
    o_ref[0] = jnp.dot(s, w_ref[0], preferred_element_type=F32) + b_ref[0]


def _modulation(cond, ada_w, ada_b):
    tn = 1536
    return pl.pallas_call(
        _mod_kernel,
        grid=(DEPTH, 6 * D // tn),
        in_specs=[
            pl.BlockSpec((16, D), lambda l, n: (0, 0)),
            pl.BlockSpec((1, D, tn), lambda l, n: (l, 0, n)),
            pl.BlockSpec((1, 1, tn), lambda l, n: (l, 0, n)),
        ],
        out_specs=pl.BlockSpec((1, 16, tn), lambda l, n: (l, 0, n)),
        out_shape=jax.ShapeDtypeStruct((DEPTH, 16, 6 * D), F32),
        compiler_params=_cp(("arbitrary", "arbitrary")),
        name="modulation",
    )(cond, ada_w, ada_b.reshape(DEPTH, 1, 6 * D))


def _seg_mean_sq(x, seg):
    s = jnp.dot((x * x).astype(BF16), seg, preferred_element_type=F32)
    return s * (1.0 / NA_DH)


def _swap16(y):
    n = y.shape[-1]
    lane = lax.broadcasted_iota(jnp.int32, y.shape, 1)
    return jnp.where(lane % 32 < 16, pltpu.roll(y, n - 16, 1), pltpu.roll(y, 16, 1))


def _rope_tables(S):
    half = NA_DH // 2
    nf = half // 2
    t = np.arange(S - CTX)
    inv_freq = (ROPE_BASE ** (-np.arange(nf, dtype=np.float32) / nf)).astype(np.float32)
    ang_r = (t // GRID_W).astype(np.float32)[:, None] * inv_freq
    ang_c = (t % GRID_W).astype(np.float32)[:, None] * inv_freq
    ang = np.concatenate([ang_r, ang_r, ang_c, ang_c], axis=-1)
    sign = np.concatenate([-np.ones(nf), np.ones(nf), -np.ones(nf), np.ones(nf)]).astype(np.float32)
    cos = np.concatenate([np.ones((CTX, NA_DH), np.float32), np.cos(ang)], axis=0)
    sin = np.concatenate([np.zeros((CTX, NA_DH), np.float32), np.sin(ang) * sign], axis=0)
    return np.tile(cos, (1, NA_H)).astype(np.float32), np.tile(sin, (1, NA_H)).astype(np.float32)


NA_Q_SCALE = NA_DH ** -0.5 * LOG2E
PM_V = 5
PM_GATE0 = 6
PM_COLS = 12 * BW


def _proj_kernel(c_ref, x_ref, mod_ref, g_ref, w_ref, qg_ref, kg_ref, seg_ref, cos_ref, sin_ref,
                 p_ref, pa_ref, qr_ref, qp_ref, kr_ref):
    j = pl.program_id(1)
    x = jnp.where(j == 0, c_ref[0], x_ref[0])
    m = mod_ref[0]
    y = x * lax.rsqrt(jnp.mean(x * x, axis=-1, keepdims=True) + EPS) * g_ref[...]
    u = (y * (1.0 + m[:, D:2 * D]) + m[:, 0:D]).astype(BF16)
    r = jnp.dot(u, w_ref[...], preferred_element_type=F32)
    pa_ref[...] = r[:, :2 * BW]
    p_ref[0, :, :5 * BW] = r[:, 2 * BW:7 * BW].astype(p_ref.dtype)
    p_ref[0, :, 5 * BW:] = r[:, 9 * BW:].astype(p_ref.dtype)
    seg = seg_ref[...]
    cos = cos_ref[...]
    sin = sin_ref[...]
    q = r[:, 7 * BW:8 * BW]
    qn = q * lax.rsqrt(_seg_mean_sq(q, seg) + EPS) * qg_ref[...]
    qp_ref[0] = (qn * NA_Q_SCALE).astype(BF16)
    qr_ref[0] = ((qn * cos + _swap16(qn) * sin) * NA_Q_SCALE).astype(BF16)
    k = r[:, 8 * BW:9 * BW]
    kn = k * lax.rsqrt(_seg_mean_sq(k, seg) + EPS) * kg_ref[...]
    kr_ref[0] = (kn * cos + _swap16(kn) * sin).astype(BF16)


def _stream_specs(lat_first, first_tile=0):
    return (pl.BlockSpec((1, TM, D), lambda b, j: (b, 0, 0)),
            pl.BlockSpec((1, TM, D), lambda b, j: (b, jnp.maximum(j + first_tile - lat_first, 0), 0)))


def _projection(ctx_arr, lat_arr, lat_first, mod_l, g, w, q_g, k_g):
    B = lat_arr.shape[0]
    S = lat_arr.shape[1] + lat_first * TM
    nt = S // TM
    cos, sin = _rope_tables(S)
    seg = np.kron(np.eye(NA_H, dtype=np.float32), np.ones((NA_DH, NA_DH), np.float32))
    const = lambda shape: pl.BlockSpec(shape, lambda b, j: (0,) * len(shape))
    tok = lambda w_: pl.BlockSpec((1, TM, w_), lambda b, j: (b, j, 0))
    return pl.pallas_call(
        _proj_kernel,
        grid=(B, nt),
        in_specs=[
            *_stream_specs(lat_first),
            pl.BlockSpec((1, 1, 6 * D), lambda b, j: (_mod_row(b, j), 0, 0)),
            const((1, D)),
            pl.BlockSpec(w.shape, lambda b, j: (0, 0), pipeline_mode=pl.Buffered(1)),
            const((1, BW)), const((1, BW)), const((BW, BW)),
            pl.BlockSpec((TM, BW), lambda b, j: (j, 0)), pl.BlockSpec((TM, BW), lambda b, j: (j, 0)),
        ],
        out_specs=[tok(PM_COLS), pl.BlockSpec((TM, 2 * BW), lambda b, j: (j, b)), tok(BW), tok(BW), tok(BW)],
        out_shape=[
            jax.ShapeDtypeStruct((B, S, PM_COLS), BF16),
            jax.ShapeDtypeStruct((S, B * 2 * BW), F32),
            jax.ShapeDtypeStruct((B, S, BW), BF16),
            jax.ShapeDtypeStruct((B, S, BW), BF16),
            jax.ShapeDtypeStruct((B, S, BW), BF16),
        ],
        compiler_params=_cp(("arbitrary", "arbitrary")),
        name="projection",
    )(ctx_arr, lat_arr, mod_l, g, w, q_g, k_g, jnp.asarray(seg, BF16), jnp.asarray(cos), jnp.asarray(sin))


LRU_CW = 256
LRU_TM = 128


def _seq_tile(p, j, nt, nct):
    return jnp.where(p == 0, j, jnp.where(j < nct, nct - 1 - j, nt - 1 - (j - nct)))


def _softplus(x):
    return jnp.maximum(x, 0.0) + jnp.log1p(jnp.exp(-jnp.abs(x)))


def _gelu_tanh(x):
    return 0.5 * x * (1.0 + jnp.tanh(np.sqrt(2.0 / np.pi) * (x + 0.044715 * (x * x * x))))


def _lru_kernel(x_ref, gate_ref, prev_ref, next_ref, cw_ref, cb_ref, wa_ref, wx_ref, ba_ref, bx_ref, lam_ref,
                y_ref, hf_ref, a_ref, b_ref, carry_ref, *, nt, nct):
    TM = LRU_TM
    p = pl.program_id(1)
    j = pl.program_id(2)
    tile = _seq_tile(p, j, nt, nct)
    nb = x_ref.shape[1]

    @pl.when(j == 0)
    def _():
        carry_ref[...] = jnp.zeros_like(carry_ref)

    x = x_ref[...]
    has_prev = jnp.logical_and(tile != 0, tile != nct)
    has_next = jnp.logical_and(tile != nct - 1, tile != nt - 1)
    prev = jnp.where(has_prev, prev_ref[...], 0.0)
    nxt = jnp.where(has_next, next_ref[...], 0.0)
    xe = jnp.concatenate([prev, x, nxt], axis=0)
    cw = cw_ref[...]
    xc = cb_ref[...][None] + xe[0:TM] * cw[0:1][None] + xe[1:TM + 1] * cw[1:2][None] \
        + xe[2:TM + 2] * cw[2:3][None] + xe[3:TM + 3] * cw[3:4][None]
    xc2 = xc.reshape(TM * nb, LRU_CW)
    xb = xc2.astype(BF16)
    r = jax.nn.sigmoid(jnp.dot(xb, wa_ref[0, 0], preferred_element_type=F32) + ba_ref[0])
    i = jax.nn.sigmoid(jnp.dot(xb, wx_ref[0, 0], preferred_element_type=F32) + bx_ref[0])
    log_a = (-LRU_C) * r * _softplus(-lam_ref[0])
    a = jnp.exp(log_a)
    bc = jnp.sqrt(1.0 - a * a) * (i * xc2)
    a_ref[...] = a.reshape(TM, nb, LRU_CW)
    b_ref[...] = bc.reshape(TM, nb, LRU_CW)

    def step(t, h):
        h = a_ref[t] * h + b_ref[t]
        b_ref[t] = h
        return h

    @pl.when(p == 0)
    def _():
        carry_ref[...] = lax.fori_loop(0, TM, step, carry_ref[...], unroll=8)
        hf_ref[pl.ds(pl.multiple_of(tile * TM, TM), TM)] = b_ref[...]

    @pl.when(p == 1)
    def _():
        carry_ref[...] = lax.fori_loop(0, TM, lambda s, h: step(TM - 1 - s, h), carry_ref[...], unroll=8)
        hf = hf_ref[pl.ds(pl.multiple_of(tile * TM, TM), TM)]
        y_ref[...] = (hf + b_ref[...]) * _gelu_tanh(gate_ref[...])


def _rglru(pa, conv_w, conv_b, wa_bd, wx_bd, ba, bx, lam):
    S, B, _ = pa.shape
    TM = LRU_TM
    nt = S // TM
    nct = CTX // TM
    ncw = BW // LRU_CW
    tile = functools.partial(_seq_tile, nt=nt, nct=nct)
    vec = lambda: pl.BlockSpec((1, 1, LRU_CW), lambda c, p, j: (p, 0, c))
    return pl.pallas_call(
        functools.partial(_lru_kernel, nt=nt, nct=nct),
        grid=(ncw, 2, nt),
        in_specs=[
            pl.BlockSpec((TM, B, LRU_CW), lambda c, p, j: (tile(p, j), 0, c)),
            pl.BlockSpec((TM, B, LRU_CW), lambda c, p, j: (tile(p, j), 0, ncw + c)),
            pl.BlockSpec((2, B, LRU_CW), lambda c, p, j: (jnp.maximum(tile(p, j) * (TM // 2) - 1, 0), 0, c)),
            pl.BlockSpec((1, B, LRU_CW), lambda c, p, j: (jnp.minimum((tile(p, j) + 1) * TM, S - 1), 0, c)),
            pl.BlockSpec((4, LRU_CW), lambda c, p, j: (0, c)),
            pl.BlockSpec((1, LRU_CW), lambda c, p, j: (0, c)),
            pl.BlockSpec((1, 1, LRU_CW, LRU_CW), lambda c, p, j: (p, c, 0, 0)),
            pl.BlockSpec((1, 1, LRU_CW, LRU_CW), lambda c, p, j: (p, c, 0, 0)),
            vec(), vec(), vec(),
        ],
        out_specs=pl.BlockSpec((TM, B, LRU_CW), lambda c, p, j: (jnp.where(p == 0, nct - 1, tile(p, j)), 0, c)),
        out_shape=jax.ShapeDtypeStruct((S, B, BW), F32),
        scratch_shapes=[
            pltpu.VMEM((S, B, LRU_CW), F32),
            pltpu.VMEM((TM, B, LRU_CW), F32),
            pltpu.VMEM((TM, B, LRU_CW), F32),
            pltpu.VMEM((B, LRU_CW), F32),
        ],
        compiler_params=_cp(("arbitrary", "arbitrary", "arbitrary")),
        name="rglru",
    )(pa, pa, pa, pa, conv_w, conv_b, wa_bd, wx_bd, ba, bx, lam)


def _block_diag(w):
    per = LRU_CW // LRU_BS
    w = w.reshape(2, BW // LRU_CW, per, LRU_BS, LRU_BS)
    eye = jnp.eye(per, dtype=w.dtype)
    out = jnp.einsum('dcnio,nm->dcnimo', w, eye)
    return out.reshape(2, BW // LRU_CW, LRU_CW, LRU_CW)


HG_NLEV = 6
HG_ROWS = (HG_NLEV + 2) * HG_C


def _hgrn_constants():
    C = HG_C
    W = np.zeros((2, HG_NLEV + 2, C, C), np.float32)
    M = np.zeros((2, HG_NLEV + 1, C, C), np.float32)
    for lv in range(HG_NLEV):
        m = C >> (lv + 1)
        for t in range(C):
            blk = t // m
            if blk % 2 == 1:
                p = blk * m
                W[0, lv, t, p:t + 1] = 1.0
                W[1, lv, t, p:t] = 1.0
            else:
                p = (blk + 1) * m
                W[0, lv, t, t + 1:p] = 1.0
                W[1, lv, t, t:p] = 1.0
        for t in range(C):
            for s in range(C):
                if (t // m) % 2 == 1 and s // m == t // m - 1:
                    M[0, lv, t, s] = 1.0
                    M[1, lv, s, t] = 1.0
    for t in range(C):
        W[0, HG_NLEV, t, :t + 1] = 1.0
        W[1, HG_NLEV, t, t:] = 1.0
        W[0, HG_NLEV + 1, t, t + 1:] = 1.0
        W[1, HG_NLEV + 1, t, :t] = 1.0
        M[:, HG_NLEV, t, t] = 1.0
    W = W.reshape(2, HG_ROWS, C)
    lev = np.full((2, C, C), -1, np.int32)
    for lv in range(HG_NLEV + 1):
        lev[M[:, lv] > 0.5] = lv
    return np.concatenate([W, W], axis=-1), lev


def _hgrn_kernel(q_ref, f_ref, v_ref, og_ref, lbl_ref, ng_ref, w_ref, m_ref, y_ref, of_ref, st_ref, *, nt, layer):
    p = pl.program_id(1)
    j = pl.program_id(2)
    tile = _seq_tile(p, j, nt, CTX // TM)
    C = HG_C
    nchunk = TM // C

    @pl.when(j == 0)
    def _():
        st_ref[...] = jnp.zeros_like(st_ref)

    logits = lbl_ref[0]
    e = jnp.exp(logits - jnp.max(logits, axis=0, keepdims=True))
    soft = e / jnp.sum(e, axis=0, keepdims=True)
    lb = jnp.sum(soft[0:layer + 1], axis=0, keepdims=True) - soft[0:1]

    wmat = w_ref[0]
    lev = m_ref[0]
    row0 = pl.multiple_of(tile * TM, TM)
    inter = HG_NLEV * C

    def direction(backward):
        f = lb + (1.0 - lb) * jax.nn.sigmoid(f_ref[0].astype(F32))
        g = jnp.log(f) * LOG2E
        k = (1.0 - f).astype(BF16)
        qr = q_ref[0].astype(F32)
        q = (qr * jax.nn.sigmoid(qr)).astype(BF16)
        v = v_ref[0]
        g1 = g.astype(BF16)
        g2 = (g - g1.astype(F32)).astype(BF16)
        g_split = jnp.concatenate(
            [jnp.concatenate([g1[c * C:(c + 1) * C], g2[c * C:(c + 1) * C]], axis=0) for c in range(nchunk)], axis=1)
        ex32 = jnp.exp2(jnp.dot(wmat, g_split, preferred_element_type=F32))
        ex = ex32.astype(BF16)
        dec_row = inter if backward else inter + C - 1
        dec_all = ex32[dec_row:dec_row + 1]
        intra, q_in, upd = {}, {}, {}
        for c in range(nchunk):
            rows = slice(c * C, (c + 1) * C)
            for h in range(HG_H):
                hs = slice(h * HG_DK, (h + 1) * HG_DK)
                qh, kh, vh = q[rows, hs], k[rows, hs], v[rows, hs]
                eh = ex[:, c * BW + h * HG_DK:c * BW + (h + 1) * HG_DK].reshape(HG_NLEV + 2, C, HG_DK)
                qs = jnp.concatenate([qh[None] * eh[:HG_NLEV], qh[None]], axis=0)
                ks = jnp.concatenate([kh[None] * eh[:HG_NLEV], kh[None]], axis=0)
                att_l = jnp.einsum('lqd,lkd->lqk', qs, ks, preferred_element_type=F32)
                att = jnp.zeros((C, C), F32)
                for lv in range(HG_NLEV + 1):
                    att = jnp.where(lev == lv, att_l[lv], att)
                intra[c, h] = jnp.dot(att.astype(BF16), vh, preferred_element_type=F32)
                q_in[c, h] = qh * eh[HG_NLEV]
                upd[c, h] = lax.dot_general(vh, kh * eh[HG_NLEV + 1], (((0,), (0,)), ((), ())),
                                            preferred_element_type=F32)
        st = [st_ref[h] for h in range(HG_H)]
        outs = [None] * nchunk
        for c in (reversed(range(nchunk)) if backward else range(nchunk)):
            row = []
            for h in range(HG_H):
                row.append(intra[c, h] + lax.dot_general(q_in[c, h], st[h].astype(BF16), (((1,), (1,)), ((), ())),
                                                         preferred_element_type=F32))
                st[h] = dec_all[:, c * BW + h * HG_DK:c * BW + (h + 1) * HG_DK] * st[h] + upd[c, h]
            outs[c] = jnp.concatenate(row, axis=-1)
        for h in range(HG_H):
            st_ref[h] = st[h]
        return jnp.concatenate(outs, axis=0)

    @pl.when(p == 0)
    def _():
        of_ref[pl.ds(row0, TM), :] = direction(False)

    @pl.when(p == 1)
    def _():
        o = direction(True) + of_ref[pl.ds(row0, TM), :]
        ys = []
        for h in range(HG_H):
            oh = o[:, h * HG_DK:(h + 1) * HG_DK]
            ys.append(oh * lax.rsqrt(jnp.mean(oh * oh, axis=-1, keepdims=True) + EPS))
        yn = jnp.concatenate(ys, axis=-1) * ng_ref[...]
        y_ref[0] = (yn * jax.nn.sigmoid(og_ref[0].astype(F32))).astype(y_ref.dtype)


def _hgrn2(pm, lb_logits, norm_g, layer):
    B, S, _ = pm.shape
    nt = S // TM
    tile = functools.partial(_seq_tile, nt=nt, nct=CTX // TM)
    w_np, m_np = _hgrn_constants()
    w_c = jnp.asarray(w_np, BF16)
    m_c = jnp.asarray(m_np, jnp.int32)
    col = lambda cb: pl.BlockSpec((1, TM, BW), lambda b, p, j: (b, tile(p, j), cb))
    return pl.pallas_call(
        functools.partial(_hgrn_kernel, nt=nt, layer=layer),
        grid=(B, 2, nt),
        in_specs=[
            col(0),
            pl.BlockSpec((1, TM, BW), lambda b, p, j: (b, tile(p, j), 1 + p)),
            col(3), col(4),
            pl.BlockSpec((1, DEPTH, BW), lambda b, p, j: (p, 0, 0)),
            pl.BlockSpec((1, BW), lambda b, p, j: (0, 0)),
            pl.BlockSpec((1, HG_ROWS, 2 * HG_C), lambda b, p, j: (p, 0, 0)),
            pl.BlockSpec((1, HG_C, HG_C), lambda b, p, j: (p, 0, 0)),
        ],
        out_specs=pl.BlockSpec((1, TM, BW), lambda b, p, j: (b, jnp.where(p == 0, CTX // TM - 1, tile(p, j)), 0)),
        out_shape=jax.ShapeDtypeStruct((B, S, BW), BF16),
        scratch_shapes=[
            pltpu.VMEM((S, BW), F32),
            pltpu.VMEM((HG_H, HG_DK, HG_DK), F32),
        ],
        compiler_params=_cp(("arbitrary", "arbitrary", "arbitrary")),
        name="hgrn2",
    )(pm, pm, pm, pm, lb_logits, norm_g, w_c, m_c)


NA_GH = 4
NA_GW = NA_GH * NA_DH


def _head_mask(h, n_rows):
    lane = lax.broadcasted_iota(jnp.int32, (n_rows, NA_GW), 1)
    return lane // NA_DH == h


def _softmax_pv(s_parts, v_parts):
    out = jnp.zeros((TM, NA_GW), F32)
    for h in range(NA_GH):
        parts = [s[h * TM:(h + 1) * TM] for s in s_parts]
        m = parts[0].max(axis=-1, keepdims=True)
        for s in parts[1:]:
            m = jnp.maximum(m, s.max(axis=-1, keepdims=True))
        ps = [jnp.exp2(s - m) for s in parts]
        den = ps[0].sum(axis=-1, keepdims=True)
        for pp in ps[1:]:
            den = den + pp.sum(axis=-1, keepdims=True)
        acc = jnp.zeros((TM, NA_GW), F32)
        for pp, vv in zip(ps, v_parts):
            vm = jnp.where(_head_mask(h, vv.shape[0]), vv, jnp.zeros_like(vv))
            acc = acc + jnp.dot(pp.astype(BF16), vm, preferred_element_type=F32)
        out = out + acc * (1.0 / den)
    return out.astype(BF16)


def _na_kernel(qr_ref, qp_ref, k0_ref, k1_ref, k2_ref, v0_ref, v1_ref, v2_ref, kc_ref, vc_ref, bias_ref, y_ref,
               *, with_ctx):
    step = pl.program_id(0)
    nkeys = NA_UR * GRID_W
    masks_q = [_head_mask(h, TM) for h in range(NA_GH)]

    def stack_heads(q):
        return jnp.concatenate([jnp.where(masks_q[h], q, jnp.zeros_like(q)) for h in range(NA_GH)], axis=0)

    nt_dims = (((1,), (1,)), ((), ()))

    def latent():
        for g in range(NA_H // NA_GH):
            gs = slice(g * NA_GW, (g + 1) * NA_GW)
            kw = jnp.concatenate([k0_ref[0, :, gs], k1_ref[0, :, gs], k2_ref[0, :, gs]], axis=0)
            vw = jnp.concatenate([v0_ref[0, :, gs], v1_ref[0, :, gs], v2_ref[0, :, gs]], axis=0)
            s_loc = lax.dot_general(stack_heads(qr_ref[0, :, gs]), kw, nt_dims, preferred_element_type=F32)
            s_loc = s_loc + bias_ref[0, g * NA_GH:(g + 1) * NA_GH].reshape(NA_GH * TM, nkeys)
            s_ctx = lax.dot_general(stack_heads(qp_ref[0, :, gs]), kc_ref[0, :, gs], nt_dims,
                                    preferred_element_type=F32)
            y_ref[0, :, gs] = _softmax_pv([s_loc, s_ctx], [vw, vc_ref[0, :, gs]])

    def context():
        for g in range(NA_H // NA_GH):
            gs = slice(g * NA_GW, (g + 1) * NA_GW)
            s_ctx = lax.dot_general(stack_heads(qp_ref[0, :, gs]), kc_ref[0, :, gs], nt_dims,
                                    preferred_element_type=F32)
            y_ref[0, :, gs] = _softmax_pv([s_ctx], [vc_ref[0, :, gs]])

    if with_ctx:
        pl.when(step == 0)(context)
        pl.when(step > 0)(latent)
    else:
        latent()


def _na_bias_tables(rpb):
    rows = 2048 // GRID_W
    col = np.arange(GRID_W)
    c_start = np.clip(col - WIN_C // 2, 0, GRID_W - WIN_C)
    col_ok = (col[None, :] >= c_start[:, None]) & (col[None, :] < c_start[:, None] + WIN_C)
    d_col = np.clip(col[None, :] - col[:, None], -(WIN_C - 1), WIN_C - 1) + (WIN_C - 1)
    hot = jnp.asarray(np.eye(2 * WIN_C - 1, dtype=np.float32)[d_col])
    by_col = jnp.einsum('hrc,qkc->hqrk', rpb.astype(F32), hot, precision=lax.Precision.HIGHEST) * LOG2E
    by_col = jnp.where(jnp.asarray(col_ok)[None, :, None, :], by_col, MASK_VALUE)
    masked = lambda n: jnp.full((NA_H, GRID_W, n, GRID_W), MASK_VALUE, F32)
    tabs = []
    for rb in (0, 1, rows // NA_RB - 1):
        u0 = int(np.clip(rb * NA_RB - NA_RB, 0, rows - NA_UR))
        per_row = []
        for r in range(rb * NA_RB, (rb + 1) * NA_RB):
            kr0 = int(np.clip(r - WIN_R // 2, 0, rows - WIN_R))
            d0 = kr0 - r + (WIN_R - 1)
            per_row.append(jnp.concatenate(
                [masked(kr0 - u0), by_col[:, :, d0:d0 + WIN_R], masked(u0 + NA_UR - kr0 - WIN_R)], axis=2))
        tabs.append(jnp.stack(per_row, axis=1).reshape(NA_H, NA_RB * GRID_W, NA_UR * GRID_W))
    return jnp.stack(tabs)


def _na_key_base(step_latent):
    return jnp.clip(step_latent - 1, 0, 2048 // TM - NA_UR * GRID_W // TM)


def _attention(qr, qp, kr, pm, bias, with_ctx):
    B, S, _ = qr.shape
    nlat = (S - CTX) // TM
    off = 0 if with_ctx else 1

    def qtile(t, b):
        return (b, t + off, 0)

    def ktile(i, cb=0):
        def f(t, b):
            return (b, 1 + _na_key_base(t + off - 1) + i, cb)
        return f

    def bias_idx(t, b):
        lat = t + off - 1
        return (jnp.where(lat <= 0, 0, jnp.where(lat == nlat - 1, 2, 1)), 0, 0, 0)

    blk = lambda f: pl.BlockSpec((1, TM, BW), f)
    return pl.pallas_call(
        functools.partial(_na_kernel, with_ctx=with_ctx),
        grid=(nlat + (1 if with_ctx else 0), B),
        in_specs=[blk(qtile), blk(qtile), blk(ktile(0)), blk(ktile(1)), blk(ktile(2)),
                  blk(ktile(0, PM_V)), blk(ktile(1, PM_V)), blk(ktile(2, PM_V)),
                  blk(lambda t, b: (b, 0, 0)), blk(lambda t, b: (b, 0, PM_V)),
                  pl.BlockSpec((1, NA_H, TM, NA_UR * GRID_W), bias_idx)],
        out_specs=blk(lambda t, b: (b, t, 0)),
        out_shape=jax.ShapeDtypeStruct((B, S - off * TM, BW), BF16),
        compiler_params=_cp(("arbitrary", "arbitrary")),
        name="attention",
    )(qr, qp, kr, kr, kr, pm, pm, pm, kr, pm, bias)


def _merge_ffn_kernel(c_ref, h_ref, mod_ref, ya_ref, yb_ref, yc_ref, g0_ref, g1_ref, g2_ref, wb_ref, wo_ref, ng_ref,
                      w1_ref, w2_ref, o_ref, *, first_tile):
    h_in = h_ref[0]
    if first_tile == 0:
        h_in = jnp.where(pl.program_id(1) == 0, c_ref[0], h_in)
    m = mod_ref[0]
    acc = None
    for n, (y_ref, g_ref) in enumerate(((ya_ref, g0_ref), (yb_ref, g1_ref), (yc_ref, g2_ref))):
        y = y_ref[...].reshape(TM, BW).astype(BF16)
        term = jax.nn.sigmoid(g_ref[0].astype(F32)) * jnp.dot(y, wb_ref[n], preferred_element_type=F32)
        acc = term if acc is None else acc + term
    mixed = jnp.dot(acc.astype(BF16), wo_ref[...], preferred_element_type=F32)
    x = h_in + m[:, 2 * D:3 * D] * mixed
    y = x * lax.rsqrt(jnp.mean(x * x, axis=-1, keepdims=True) + EPS) * ng_ref[...]
    u = (y * (1.0 + m[:, 4 * D:5 * D]) + m[:, 3 * D:4 * D]).astype(BF16)
    a = jnp.maximum(jnp.dot(u, w1_ref[...], preferred_element_type=F32), 0.0)
    out = jnp.dot((a * a).astype(BF16), w2_ref[...], preferred_element_type=F32)
    o_ref[0] = x + m[:, 5 * D:6 * D] * out


def _merge_ffn(ctx_arr, lat_arr, lat_first, mod_l, ya, yb, yc, pm, w_branch, w_out, norm_g, w1, w2, first_tile):
    B, S, _ = yb.shape
    nt = S // TM - first_tile
    tok = lambda w: pl.BlockSpec((1, TM, w), lambda b, j: (b, j + first_tile, 0))
    gate = lambda n: pl.BlockSpec((1, TM, D), lambda b, j: (b, j + first_tile, PM_GATE0 * BW // D + n))
    const = lambda shape: pl.BlockSpec(shape, lambda b, j: (0,) * len(shape), pipeline_mode=pl.Buffered(1))
    return pl.pallas_call(
        functools.partial(_merge_ffn_kernel, first_tile=first_tile),
        grid=(B, nt),
        in_specs=[
            *_stream_specs(lat_first, first_tile),
            pl.BlockSpec((1, 1, 6 * D), lambda b, j: (_mod_row(b, j + first_tile), 0, 0)),
            pl.BlockSpec((TM, BW), lambda b, j: (j + first_tile, b)),
            tok(BW),
            pl.BlockSpec((1, TM, BW), lambda b, j: (b, j, 0)),
            gate(0), gate(1), gate(2),
            const((3, BW, D)), const((D, D)), const((1, D)), const((D, D_FF)), const((D_FF, D)),
        ],
        out_specs=pl.BlockSpec((1, TM, D), lambda b, j: (b, j, 0)),
        out_shape=jax.ShapeDtypeStruct((B, nt * TM, D), F32),
        compiler_params=_cp(("arbitrary", "arbitrary")),
        name="merge_ffn",
    )(ctx_arr, lat_arr, mod_l, ya, yb, yc, pm, pm, pm, w_branch, w_out, norm_g, w1, w2)


def kernel(x, c, ctx, c_ctx, ada_w, ada_b, norm1_g, norm2_g, w_in, conv_w, conv_b, lru_wa, lru_ba, lru_wx, lru_bx,
           lru_lambda, hg_lb_logits, hg_norm_g, na_q_norm_g, na_k_norm_g, na_rpb, w_branch, w_out, ffn_w1, ffn_w2):
    B, T, _ = x.shape
    S = CTX + T
    cond = jnp.zeros((16, D), F32).at[:B].set(c).at[8].set(c_ctx)
    mod = _modulation(cond, ada_w, ada_b)
    stream = (ctx, x, 1)
    for l in range(DEPTH):
        last = l == DEPTH - 1
        mod_l = mod[l].reshape(16, 1, 6 * D)
        pm, pa, qr, qp, kr = _projection(*stream, mod_l, norm1_g[l][None], w_in[l].astype(BF16),
                                         jnp.tile(na_q_norm_g[l], NA_H)[None], jnp.tile(na_k_norm_g[l], NA_H)[None])

        ya = _rglru(pa.reshape(S, B, 2 * BW), conv_w[l], conv_b[l][None],
                    _block_diag(lru_wa[l]).astype(BF16), _block_diag(lru_wx[l]).astype(BF16),
                    lru_ba[l][:, None], lru_bx[l][:, None], lru_lambda[l][:, None])
        yb = _hgrn2(pm, hg_lb_logits, jnp.tile(hg_norm_g[l], HG_H)[None], l)
        yc = _attention(qr, qp, kr, pm, _na_bias_tables(na_rpb[l]), with_ctx=not last)

        h = _merge_ffn(*stream, mod_l, ya.reshape(S, B * BW), yb, yc, pm, w_branch[l].astype(BF16),
                       w_out[l].astype(BF16), norm2_g[l][None], ffn_w1[l].astype(BF16), ffn_w2[l].astype(BF16),
                       1 if last else 0)
        stream = (h, h, 0)
    return h
```

```python
import functools

import numpy as np
import jax
import jax.numpy as jnp
from jax import lax
from jax.experimental import pallas as pl
from jax.experimental.pallas import tpu as pltpu

F32 = jnp.float32
BF16 = jnp.bfloat16

D = 1024
DEPTH = 2
GRID_W = 64
CTX = 256
BW = 512
LRU_NB = 8
LRU_BS = BW // LRU_NB
LRU_C = 8.0
HG_H = 4
HG_DK = 128
HG_C = 64
NA_H = 8
NA_DH = 64
WIN_R = 8
WIN_C = 16
ROPE_BASE = 10000.0
MASK_VALUE = -1e30
D_FF = 4 * D
EPS = 1e-6
TM = 256
NA_RB = 4
NA_UR = 12
VMEM_LIMIT = 56 * 1024 * 1024
LOG2E = 1.4426950408889634


def _cp(sem):
    return pltpu.CompilerParams(dimension_semantics=sem, vmem_limit_bytes=VMEM_LIMIT)


def _mod_row(b, j):
    return jnp.where(j == 0, 8, b)


def _mod_kernel(c_ref, w_ref, b_ref, o_ref):
    c = c_ref[...]
    s = c * jax.nn.sigmoid(c)
    o_ref[0] = jnp.dot(s, w_ref[0], preferred_element_type=F32) + b_ref[0]


def _modulation(cond, ada_w, ada_b):
    tn = 1536
    return pl.pallas_call(
        _mod_kernel,
        grid=(DEPTH, 6 * D // tn),
        in_specs=[
            pl.BlockSpec((16, D), lambda l, n: (0, 0)),
            pl.BlockSpec((1, D, tn), lambda l, n: (l, 0, n)),
            pl.BlockSpec((1, 1, tn), lambda l, n: (l, 0, n)),
        ],
        out_specs=pl.BlockSpec((1, 16, tn), lambda l, n: (l, 0, n)),
        out_shape=jax.ShapeDtypeStruct((DEPTH, 16, 6 * D), F32),
        compiler_params=_cp(("arbitrary", "arbitrary")),
        name="modulation",
    )(cond, ada_w, ada_b.reshape(DEPTH, 1, 6 * D))


def _seg_mean_sq(x, seg):
    s = jnp.dot((x * x).astype(BF16), seg, preferred_element_type=F32)
    return s * (1.0 / NA_DH)


def _swap16(y):
    n = y.shape[-1]
    lane = lax.broadcasted_iota(jnp.int32, y.shape, 1)
    return jnp.where(lane % 32 < 16, pltpu.roll(y, n - 16, 1), pltpu.roll(y, 16, 1))


def _rope_tables(S):
    half = NA_DH // 2
    nf = half // 2
    t = np.arange(S - CTX)
    inv_freq = (ROPE_BASE ** (-np.arange(nf, dtype=np.float32) / nf)).astype(np.float32)
    ang_r = (t // GRID_W).astype(np.float32)[:, None] * inv_freq
    ang_c = (t % GRID_W).astype(np.float32)[:, None] * inv_freq
    ang = np.concatenate([ang_r, ang_r, ang_c, ang_c], axis=-1)
    sign = np.concatenate([-np.ones(nf), np.ones(nf), -np.ones(nf), np.ones(nf)]).astype(np.float32)
    cos = np.concatenate([np.ones((CTX, NA_DH), np.float32), np.cos(ang)], axis=0)
    sin = np.concatenate([np.zeros((CTX, NA_DH), np.float32), np.sin(ang) * sign], axis=0)
    return np.tile(cos, (1, NA_H)).astype(np.float32), np.tile(sin, (1, NA_H)).astype(np.float32)


NA_Q_SCALE = NA_DH ** -0.5 * LOG2E
PM_V = 5
PM_GATE0 = 6
PM_COLS = 12 * BW


def _proj_kernel(c_ref, x_ref, mod_ref, g_ref, w_ref, qg_ref, kg_ref, seg_ref, cos_ref, sin_ref,
                 p_ref, pa_ref, qr_ref, qp_ref, kr_ref):
    j = pl.program_id(1)
    x = jnp.where(j == 0, c_ref[0], x_ref[0])
    m = mod_ref[0]
    y = x * lax.rsqrt(jnp.mean(x * x, axis=-1, keepdims=True) + EPS) * g_ref[...]
    u = (y * (1.0 + m[:, D:2 * D]) + m[:, 0:D]).astype(BF16)
    dot = lambda lo, hi: jnp.dot(u, w_ref[:, lo * BW:hi * BW], preferred_element_type=F32)
    seg = seg_ref[...]
    cos = cos_ref[...]
    sin = sin_ref[...]
    q = dot(7, 8)
    qn = q * lax.rsqrt(_seg_mean_sq(q, seg) + EPS) * qg_ref[...]
    qp_ref[0] = (qn * NA_Q_SCALE).astype(BF16)
    qr_ref[0] = ((qn * cos + _swap16(qn) * sin) * NA_Q_SCALE).astype(BF16)
    k = dot(8, 9)
    kn = k * lax.rsqrt(_seg_mean_sq(k, seg) + EPS) * kg_ref[...]
    kr_ref[0] = (kn * cos + _swap16(kn) * sin).astype(BF16)
    pa_ref[...] = dot(0, 2)
    p_ref[0, :, :5 * BW] = dot(2, 7).astype(p_ref.dtype)
    p_ref[0, :, 5 * BW:] = dot(9, 16).astype(p_ref.dtype)


def _stream_specs(lat_first, first_tile=0):
    return (pl.BlockSpec((1, TM, D), lambda b, j: (b, 0, 0)),
            pl.BlockSpec((1, TM, D), lambda b, j: (b, jnp.maximum(j + first_tile - lat_first, 0), 0)))


def _projection(ctx_arr, lat_arr, lat_first, mod_l, g, w, l, q_g, k_g):
    B = lat_arr.shape[0]
    S = lat_arr.shape[1] + lat_first * TM
    nt = S // TM
    cos, sin = _rope_tables(S)
    seg = np.kron(np.eye(NA_H, dtype=np.float32), np.ones((NA_DH, NA_DH), np.float32))
    const = lambda shape: pl.BlockSpec(shape, lambda b, j: (0,) * len(shape))
    tok = lambda w_: pl.BlockSpec((1, TM, w_), lambda b, j: (b, j, 0))
    return pl.pallas_call(
        _proj_kernel,
        grid=(B, nt),
        in_specs=[
            *_stream_specs(lat_first),
            pl.BlockSpec((1, 1, 6 * D), lambda b, j: (_mod_row(b, j), 0, 0)),
            const((1, D)),
            pl.BlockSpec((None,) + w.shape[1:], lambda b, j: (l, 0, 0), pipeline_mode=pl.Buffered(1)),
            const((1, BW)), const((1, BW)), const((BW, BW)),
            pl.BlockSpec((TM, BW), lambda b, j: (j, 0)), pl.BlockSpec((TM, BW), lambda b, j: (j, 0)),
        ],
        out_specs=[tok(PM_COLS), pl.BlockSpec((TM, 2 * BW), lambda b, j: (j, b)), tok(BW), tok(BW), tok(BW)],
        out_shape=[
            jax.ShapeDtypeStruct((B, S, PM_COLS), BF16),
            jax.ShapeDtypeStruct((S, B * 2 * BW), F32),
            jax.ShapeDtypeStruct((B, S, BW), BF16),
            jax.ShapeDtypeStruct((B, S, BW), BF16),
            jax.ShapeDtypeStruct((B, S, BW), BF16),
        ],
        compiler_params=_cp(("arbitrary", "arbitrary")),
        name="projection",
    )(ctx_arr, lat_arr, mod_l, g, w, q_g, k_g, jnp.asarray(seg, BF16), jnp.asarray(cos), jnp.asarray(sin))


LRU_CW = 256
LRU_TM = 128


def _seq_tile(p, j, nt, nct):
    return jnp.where(p == 0, j, jnp.where(j < nct, nct - 1 - j, nt - 1 - (j - nct)))


def _softplus(x):
    return jnp.maximum(x, 0.0) + jnp.log1p(jnp.exp(-jnp.abs(x)))


def _gelu_tanh(x):
    return 0.5 * x * (1.0 + jnp.tanh(np.sqrt(2.0 / np.pi) * (x + 0.044715 * (x * x * x))))


def _lru_kernel(x_ref, gate_ref, prev_ref, next_ref, cw_ref, cb_ref, wa_ref, wx_ref, ba_ref, bx_ref, lam_ref,
                y_ref, hf_ref, a_ref, b_ref, carry_ref, *, nt, nct):
    TM = LRU_TM
    p = pl.program_id(1)
    j = pl.program_id(2)
    tile = _seq_tile(p, j, nt, nct)
    nb = x_ref.shape[1]

    @pl.when(j == 0)
    def _():
        carry_ref[...] = jnp.zeros_like(carry_ref)

    x = x_ref[...]
    has_prev = jnp.logical_and(tile != 0, tile != nct)
    has_next = jnp.logical_and(tile != nct - 1, tile != nt - 1)
    prev = jnp.where(has_prev, prev_ref[...], 0.0)
    nxt = jnp.where(has_next, next_ref[...], 0.0)
    xe = jnp.concatenate([prev, x, nxt], axis=0)
    cw = cw_ref[...]
    xc = cb_ref[...][None] + xe[0:TM] * cw[0:1][None] + xe[1:TM + 1] * cw[1:2][None] \
        + xe[2:TM + 2] * cw[2:3][None] + xe[3:TM + 3] * cw[3:4][None]
    xc2 = xc.reshape(TM * nb, LRU_CW)
    xb = xc2.astype(BF16)
    r = jax.nn.sigmoid(jnp.dot(xb, wa_ref[0, 0], preferred_element_type=F32) + ba_ref[0])
    i = jax.nn.sigmoid(jnp.dot(xb, wx_ref[0, 0], preferred_element_type=F32) + bx_ref[0])
    log_a = (-LRU_C) * r * _softplus(-lam_ref[0])
    a = jnp.exp(log_a)
    bc = jnp.sqrt(1.0 - a * a) * (i * xc2)
    a_ref[...] = a.reshape(TM, nb, LRU_CW)
    b_ref[...] = bc.reshape(TM, nb, LRU_CW)

    def step(t, h):
        h = a_ref[t] * h + b_ref[t]
        b_ref[t] = h
        return h

    @pl.when(p == 0)
    def _():
        carry_ref[...] = lax.fori_loop(0, TM, step, carry_ref[...], unroll=8)
        hf_ref[pl.ds(pl.multiple_of(tile * TM, TM), TM)] = b_ref[...]

    @pl.when(p == 1)
    def _():
        carry_ref[...] = lax.fori_loop(0, TM, lambda s, h: step(TM - 1 - s, h), carry_ref[...], unroll=8)
        hf = hf_ref[pl.ds(pl.multiple_of(tile * TM, TM), TM)]
        y_ref[...] = (hf + b_ref[...]) * _gelu_tanh(gate_ref[...])


def _rglru(pa, conv_w, conv_b, wa_bd, wx_bd, ba, bx, lam):
    S, B, _ = pa.shape
    TM = LRU_TM
    nt = S // TM
    nct = CTX // TM
    ncw = BW // LRU_CW
    tile = functools.partial(_seq_tile, nt=nt, nct=nct)
    vec = lambda: pl.BlockSpec((1, 1, LRU_CW), lambda c, p, j: (p, 0, c))
    return pl.pallas_call(
        functools.partial(_lru_kernel, nt=nt, nct=nct),
        grid=(ncw, 2, nt),
        in_specs=[
            pl.BlockSpec((TM, B, LRU_CW), lambda c, p, j: (tile(p, j), 0, c)),
            pl.BlockSpec((TM, B, LRU_CW), lambda c, p, j: (tile(p, j), 0, ncw + c)),
            pl.BlockSpec((2, B, LRU_CW), lambda c, p, j: (jnp.maximum(tile(p, j) * (TM // 2) - 1, 0), 0, c)),
            pl.BlockSpec((1, B, LRU_CW), lambda c, p, j: (jnp.minimum((tile(p, j) + 1) * TM, S - 1), 0, c)),
            pl.BlockSpec((4, LRU_CW), lambda c, p, j: (0, c)),
            pl.BlockSpec((1, LRU_CW), lambda c, p, j: (0, c)),
            pl.BlockSpec((1, 1, LRU_CW, LRU_CW), lambda c, p, j: (p, c, 0, 0)),
            pl.BlockSpec((1, 1, LRU_CW, LRU_CW), lambda c, p, j: (p, c, 0, 0)),
            vec(), vec(), vec(),
        ],
        out_specs=pl.BlockSpec((TM, B, LRU_CW), lambda c, p, j: (jnp.where(p == 0, nct - 1, tile(p, j)), 0, c)),
        out_shape=jax.ShapeDtypeStruct((S, B, BW), F32),
        scratch_shapes=[
            pltpu.VMEM((S, B, LRU_CW), F32),
            pltpu.VMEM((TM, B, LRU_CW), F32),
            pltpu.VMEM((TM, B, LRU_CW), F32),
            pltpu.VMEM((B, LRU_CW), F32),
        ],
        compiler_params=_cp(("arbitrary", "arbitrary", "arbitrary")),
        name="rglru",
    )(pa, pa, pa, pa, conv_w, conv_b, wa_bd, wx_bd, ba, bx, lam)


def _block_diag(w):
    per = LRU_CW // LRU_BS
    w = w.reshape(2, BW // LRU_CW, per, LRU_BS, LRU_BS)
    eye = jnp.eye(per, dtype=w.dtype)
    out = jnp.einsum('dcnio,nm->dcnimo', w, eye)
    return out.reshape(2, BW // LRU_CW, LRU_CW, LRU_CW)


HG_NLEV = 6
HG_ROWS = (HG_NLEV + 2) * HG_C


def _hgrn_constants():
    C = HG_C
    W = np.zeros((2, HG_NLEV + 2, C, C), np.float32)
    M = np.zeros((2, HG_NLEV + 1, C, C), np.float32)
    for lv in range(HG_NLEV):
        m = C >> (lv + 1)
        for t in range(C):
            blk = t // m
            if blk % 2 == 1:
                p = blk * m
                W[0, lv, t, p:t + 1] = 1.0
                W[1, lv, t, p:t] = 1.0
            else:
                p = (blk + 1) * m
                W[0, lv, t, t + 1:p] = 1.0
                W[1, lv, t, t:p] = 1.0
        for t in range(C):
            for s in range(C):
                if (t // m) % 2 == 1 and s // m == t // m - 1:
                    M[0, lv, t, s] = 1.0
                    M[1, lv, s, t] = 1.0
    for t in range(C):
        W[0, HG_NLEV, t, :t + 1] = 1.0
        W[1, HG_NLEV, t, t:] = 1.0
        W[0, HG_NLEV + 1, t, t + 1:] = 1.0
        W[1, HG_NLEV + 1, t, :t] = 1.0
        M[:, HG_NLEV, t, t] = 1.0
    W = W.reshape(2, HG_ROWS, C)
    lev = np.full((2, C, C), -1, np.int32)
    for lv in range(HG_NLEV + 1):
        lev[M[:, lv] > 0.5] = lv
    return np.concatenate([W, W], axis=-1), lev


def _hgrn_kernel(q_ref, f_ref, v_ref, og_ref, lbl_ref, ng_ref, w_ref, m_ref, y_ref, of_ref, st_ref, *, nt, layer):
    p = pl.program_id(1)
    j = pl.program_id(2)
    tile = _seq_tile(p, j, nt, CTX // TM)
    C = HG_C
    nchunk = TM // C

    @pl.when(j == 0)
    def _():
        st_ref[...] = jnp.zeros_like(st_ref)

    logits = lbl_ref[0]
    e = jnp.exp(logits - jnp.max(logits, axis=0, keepdims=True))
    soft = e / jnp.sum(e, axis=0, keepdims=True)
    lb = jnp.sum(soft[0:layer + 1], axis=0, keepdims=True) - soft[0:1]

    wmat = w_ref[0]
    lev = m_ref[0]
    row0 = pl.multiple_of(tile * TM, TM)
    inter = HG_NLEV * C

    def direction(backward):
        f = lb + (1.0 - lb) * jax.nn.sigmoid(f_ref[0].astype(F32))
        g = jnp.log(f) * LOG2E
        k = (1.0 - f).astype(BF16)
        qr = q_ref[0].astype(F32)
        q = (qr * jax.nn.sigmoid(qr)).astype(BF16)
        v = v_ref[0]
        g1 = g.astype(BF16)
        g2 = (g - g1.astype(F32)).astype(BF16)
        g_split = jnp.concatenate(
            [jnp.concatenate([g1[c * C:(c + 1) * C], g2[c * C:(c + 1) * C]], axis=0) for c in range(nchunk)], axis=1)
        ex32 = jnp.exp2(jnp.dot(wmat, g_split, preferred_element_type=F32))
        ex = ex32.astype(BF16)
        dec_row = inter if backward else inter + C - 1
        dec_all = ex32[dec_row:dec_row + 1]
        intra, q_in, upd = {}, {}, {}
        for c in range(nchunk):
            rows = slice(c * C, (c + 1) * C)
            for h in range(HG_H):
                hs = slice(h * HG_DK, (h + 1) * HG_DK)
                qh, kh, vh = q[rows, hs], k[rows, hs], v[rows, hs]
                eh = ex[:, c * BW + h * HG_DK:c * BW + (h + 1) * HG_DK].reshape(HG_NLEV + 2, C, HG_DK)
                qs = jnp.concatenate([qh[None] * eh[:HG_NLEV], qh[None]], axis=0)
                ks = jnp.concatenate([kh[None] * eh[:HG_NLEV], kh[None]], axis=0)
                att_l = jnp.einsum('lqd,lkd->lqk', qs, ks, preferred_element_type=F32)
                att = jnp.zeros((C, C), F32)
                for lv in range(HG_NLEV + 1):
                    att = jnp.where(lev == lv, att_l[lv], att)
                intra[c, h] = jnp.dot(att.astype(BF16), vh, preferred_element_type=F32)
                q_in[c, h] = qh * eh[HG_NLEV]
                upd[c, h] = lax.dot_general(vh, kh * eh[HG_NLEV + 1], (((0,), (0,)), ((), ())),
                                            preferred_element_type=F32)
        st = [st_ref[h] for h in range(HG_H)]
        outs = [None] * nchunk
        for c in (reversed(range(nchunk)) if backward else range(nchunk)):
            row = []
            for h in range(HG_H):
                row.append(intra[c, h] + lax.dot_general(q_in[c, h], st[h].astype(BF16), (((1,), (1,)), ((), ())),
                                                         preferred_element_type=F32))
                st[h] = dec_all[:, c * BW + h * HG_DK:c * BW + (h + 1) * HG_DK] * st[h] + upd[c, h]
            outs[c] = jnp.concatenate(row, axis=-1)
        for h in range(HG_H):
            st_ref[h] = st[h]
        return jnp.concatenate(outs, axis=0)

    @pl.when(p == 0)
    def _():
        of_ref[pl.ds(row0, TM), :] = direction(False)

    @pl.when(p == 1)
    def _():
        o = direction(True) + of_ref[pl.ds(row0, TM), :]
        ys = []
        for h in range(HG_H):
            oh = o[:, h * HG_DK:(h + 1) * HG_DK]
            ys.append(oh * lax.rsqrt(jnp.mean(oh * oh, axis=-1, keepdims=True) + EPS))
        yn = jnp.concatenate(ys, axis=-1) * ng_ref[...]
        y_ref[0] = (yn * jax.nn.sigmoid(og_ref[0].astype(F32))).astype(y_ref.dtype)


def _hgrn2(pm, lb_logits, norm_g, layer):
    B, S, _ = pm.shape
    nt = S // TM
    tile = functools.partial(_seq_tile, nt=nt, nct=CTX // TM)
    w_np, m_np = _hgrn_constants()
    w_c = jnp.asarray(w_np, BF16)
    m_c = jnp.asarray(m_np, jnp.int32)
    col = lambda cb: pl.BlockSpec((1, TM, BW), lambda b, p, j: (b, tile(p, j), cb))
    return pl.pallas_call(
        functools.partial(_hgrn_kernel, nt=nt, layer=layer),
        grid=(B, 2, nt),
        in_specs=[
            col(0),
            pl.BlockSpec((1, TM, BW), lambda b, p, j: (b, tile(p, j), 1 + p)),
            col(3), col(4),
            pl.BlockSpec((1, DEPTH, BW), lambda b, p, j: (p, 0, 0)),
            pl.BlockSpec((1, BW), lambda b, p, j: (0, 0)),
            pl.BlockSpec((1, HG_ROWS, 2 * HG_C), lambda b, p, j: (p, 0, 0)),
            pl.BlockSpec((1, HG_C, HG_C), lambda b, p, j: (p, 0, 0)),
        ],
        out_specs=pl.BlockSpec((1, TM, BW), lambda b, p, j: (b, jnp.where(p == 0, CTX // TM - 1, tile(p, j)), 0)),
        out_shape=jax.ShapeDtypeStruct((B, S, BW), BF16),
        scratch_shapes=[
            pltpu.VMEM((S, BW), F32),
            pltpu.VMEM((HG_H, HG_DK, HG_DK), F32),
        ],
        compiler_params=_cp(("arbitrary", "arbitrary", "arbitrary")),
        name="hgrn2",
    )(pm, pm, pm, pm, lb_logits, norm_g, w_c, m_c)


NA_GH = 4
NA_GW = NA_GH * NA_DH


def _head_mask(h, n_rows):
    lane = lax.broadcasted_iota(jnp.int32, (n_rows, NA_GW), 1)
    return lane // NA_DH == h


def _softmax_pv(s_parts, v_parts):
    out = jnp.zeros((TM, NA_GW), F32)
    for h in range(NA_GH):
        parts = [s[h * TM:(h + 1) * TM] for s in s_parts]
        m = parts[0].max(axis=-1, keepdims=True)
        for s in parts[1:]:
            m = jnp.maximum(m, s.max(axis=-1, keepdims=True))
        ps = [jnp.exp2(s - m) for s in parts]
        den = ps[0].sum(axis=-1, keepdims=True)
        for pp in ps[1:]:
            den = den + pp.sum(axis=-1, keepdims=True)
        acc = jnp.zeros((TM, NA_GW), F32)
        for pp, vv in zip(ps, v_parts):
            acc = acc + jnp.dot(pp.astype(BF16), vv, preferred_element_type=F32)
        out = out + jnp.where(_head_mask(h, TM), acc * (1.0 / den), 0.0)
    return out.astype(BF16)


def _na_kernel(qr_ref, qp_ref, k0_ref, k1_ref, k2_ref, v0_ref, v1_ref, v2_ref, kc_ref, vc_ref, bias_ref, y_ref,
               *, with_ctx):
    step = pl.program_id(0)
    masks_q = [_head_mask(h, TM) for h in range(NA_GH)]

    def stack_heads(q):
        return jnp.concatenate([jnp.where(masks_q[h], q, jnp.zeros_like(q)) for h in range(NA_GH)], axis=0)

    nt_dims = (((1,), (1,)), ((), ()))

    def latent():
        for g in range(NA_H // NA_GH):
            gs = slice(g * NA_GW, (g + 1) * NA_GW)
            q_rot = stack_heads(qr_ref[0, :, gs])
            s_parts, v_parts = [], []
            for i, (k_ref, v_ref) in enumerate(((k0_ref, v0_ref), (k1_ref, v1_ref), (k2_ref, v2_ref))):
                s = lax.dot_general(q_rot, k_ref[0, :, gs], nt_dims, preferred_element_type=F32)
                bias = bias_ref[0, g * NA_GH:(g + 1) * NA_GH, :, i * TM:(i + 1) * TM]
                s_parts.append(s + bias.reshape(NA_GH * TM, TM))
                v_parts.append(v_ref[0, :, gs])
            s_parts.append(lax.dot_general(stack_heads(qp_ref[0, :, gs]), kc_ref[0, :, gs], nt_dims,
                                           preferred_element_type=F32))
            v_parts.append(vc_ref[0, :, gs])
            y_ref[0, :, gs] = _softmax_pv(s_parts, v_parts)

    def context():
        for g in range(NA_H // NA_GH):
            gs = slice(g * NA_GW, (g + 1) * NA_GW)
            s_ctx = lax.dot_general(stack_heads(qp_ref[0, :, gs]), kc_ref[0, :, gs], nt_dims,
                                    preferred_element_type=F32)
            y_ref[0, :, gs] = _softmax_pv([s_ctx], [vc_ref[0, :, gs]])

    if with_ctx:
        pl.when(step == 0)(context)
        pl.when(step > 0)(latent)
    else:
        latent()


def _na_bias_tables(rpb):
    rows = 2048 // GRID_W
    col = np.arange(GRID_W)
    c_start = np.clip(col - WIN_C // 2, 0, GRID_W - WIN_C)
    col_ok = (col[None, :] >= c_start[:, None]) & (col[None, :] < c_start[:, None] + WIN_C)
    d_col = np.clip(col[None, :] - col[:, None], -(WIN_C - 1), WIN_C - 1) + (WIN_C - 1)
    hot = jnp.asarray(np.eye(2 * WIN_C - 1, dtype=np.float32)[d_col])
    by_col = jnp.einsum('hrc,qkc->hqrk', rpb.astype(F32), hot, precision=lax.Precision.HIGHEST) * LOG2E
    by_col = jnp.where(jnp.asarray(col_ok)[None, :, None, :], by_col, MASK_VALUE)
    masked = lambda n: jnp.full((NA_H, GRID_W, n, GRID_W), MASK_VALUE, F32)
    tabs = []
    for rb in (0, 1, rows // NA_RB - 1):
        u0 = int(np.clip(rb * NA_RB - NA_RB, 0, rows - NA_UR))
        per_row = []
        for r in range(rb * NA_RB, (rb + 1) * NA_RB):
            kr0 = int(np.clip(r - WIN_R // 2, 0, rows - WIN_R))
            d0 = kr0 - r + (WIN_R - 1)
            per_row.append(jnp.concatenate(
                [masked(kr0 - u0), by_col[:, :, d0:d0 + WIN_R], masked(u0 + NA_UR - kr0 - WIN_R)], axis=2))
        tabs.append(jnp.stack(per_row, axis=1).reshape(NA_H, NA_RB * GRID_W, NA_UR * GRID_W))
    return jnp.stack(tabs)


def _na_key_base(step_latent):
    return jnp.clip(step_latent - 1, 0, 2048 // TM - NA_UR * GRID_W // TM)


def _attention(qr, qp, kr, pm, bias, with_ctx):
    B, S, _ = qr.shape
    nlat = (S - CTX) // TM
    off = 0 if with_ctx else 1

    def qtile(t, b):
        return (b, t + off, 0)

    def ktile(i, cb=0):
        def f(t, b):
            return (b, 1 + _na_key_base(t + off - 1) + i, cb)
        return f

    def bias_idx(t, b):
        lat = t + off - 1
        return (jnp.where(lat <= 0, 0, jnp.where(lat == nlat - 1, 2, 1)), 0, 0, 0)

    blk = lambda f: pl.BlockSpec((1, TM, BW), f)
    return pl.pallas_call(
        functools.partial(_na_kernel, with_ctx=with_ctx),
        grid=(nlat + (1 if with_ctx else 0), B),
        in_specs=[blk(qtile), blk(qtile), blk(ktile(0)), blk(ktile(1)), blk(ktile(2)),
                  blk(ktile(0, PM_V)), blk(ktile(1, PM_V)), blk(ktile(2, PM_V)),
                  blk(lambda t, b: (b, 0, 0)), blk(lambda t, b: (b, 0, PM_V)),
                  pl.BlockSpec((1, NA_H, TM, NA_UR * GRID_W), bias_idx)],
        out_specs=blk(lambda t, b: (b, t, 0)),
        out_shape=jax.ShapeDtypeStruct((B, S - off * TM, BW), BF16),
        compiler_params=_cp(("arbitrary", "arbitrary")),
        name="attention",
    )(qr, qp, kr, kr, kr, pm, pm, pm, kr, pm, bias)


def _merge_ffn_kernel(c_ref, h_ref, mod_ref, ya_ref, yb_ref, yc_ref, g0_ref, g1_ref, g2_ref, wb_ref, wo_ref, ng_ref,
                      w1_ref, w2_ref, o_ref, *, first_tile):
    h_in = h_ref[0]
    if first_tile == 0:
        h_in = jnp.where(pl.program_id(1) == 0, c_ref[0], h_in)
    m = mod_ref[0]
    acc = None
    for n, (y_ref, g_ref) in enumerate(((ya_ref, g0_ref), (yb_ref, g1_ref), (yc_ref, g2_ref))):
        y = y_ref[...].reshape(TM, BW).astype(BF16)
        term = jax.nn.sigmoid(g_ref[0].astype(F32)) * jnp.dot(y, wb_ref[n], preferred_element_type=F32)
        acc = term if acc is None else acc + term
    mixed = jnp.dot(acc.astype(BF16), wo_ref[...], preferred_element_type=F32)
    x = h_in + m[:, 2 * D:3 * D] * mixed
    y = x * lax.rsqrt(jnp.mean(x * x, axis=-1, keepdims=True) + EPS) * ng_ref[...]
    u = (y * (1.0 + m[:, 4 * D:5 * D]) + m[:, 3 * D:4 * D]).astype(BF16)
    a = jnp.maximum(jnp.dot(u, w1_ref[...], preferred_element_type=F32), 0.0)
    out = jnp.dot((a * a).astype(BF16), w2_ref[...], preferred_element_type=F32)
    o_ref[0] = x + m[:, 5 * D:6 * D] * out


def _merge_ffn(ctx_arr, lat_arr, lat_first, mod_l, ya, yb, yc, pm, w_branch, w_out, norm_g, w1, w2, l, first_tile):
    B, S, _ = yb.shape
    nt = S // TM - first_tile
    tok = lambda w: pl.BlockSpec((1, TM, w), lambda b, j: (b, j + first_tile, 0))
    gate = lambda n: pl.BlockSpec((1, TM, D), lambda b, j: (b, j + first_tile, PM_GATE0 * BW // D + n))
    layer = lambda shape: pl.BlockSpec((None,) + shape, lambda b, j: (l,) + (0,) * len(shape),
                                       pipeline_mode=pl.Buffered(1))
    return pl.pallas_call(
        functools.partial(_merge_ffn_kernel, first_tile=first_tile),
        grid=(B, nt),
        in_specs=[
            *_stream_specs(lat_first, first_tile),
            pl.BlockSpec((1, 1, 6 * D), lambda b, j: (_mod_row(b, j + first_tile), 0, 0)),
            pl.BlockSpec((TM, BW), lambda b, j: (j + first_tile, b)),
            tok(BW),
            pl.BlockSpec((1, TM, BW), lambda b, j: (b, j, 0)),
            gate(0), gate(1), gate(2),
            layer((3, BW, D)), layer((D, D)), pl.BlockSpec((1, D), lambda b, j: (0, 0)),
            layer((D, D_FF)), layer((D_FF, D)),
        ],
        out_specs=pl.BlockSpec((1, TM, D), lambda b, j: (b, j, 0)),
        out_shape=jax.ShapeDtypeStruct((B, nt * TM, D), F32),
        compiler_params=_cp(("arbitrary", "arbitrary")),
        name="merge_ffn",
    )(ctx_arr, lat_arr, mod_l, ya, yb, yc, pm, pm, pm, w_branch, w_out, norm_g, w1, w2)


def kernel(x, c, ctx, c_ctx, ada_w, ada_b, norm1_g, norm2_g, w_in, conv_w, conv_b, lru_wa, lru_ba, lru_wx, lru_bx,
           lru_lambda, hg_lb_logits, hg_norm_g, na_q_norm_g, na_k_norm_g, na_rpb, w_branch, w_out, ffn_w1, ffn_w2):
    B, T, _ = x.shape
    S = CTX + T
    cond = jnp.zeros((16, D), F32).at[:B].set(c).at[8].set(c_ctx)
    mod = _modulation(cond, ada_w, ada_b)
    w_in_b, w_branch_b, w_out_b = w_in.astype(BF16), w_branch.astype(BF16), w_out.astype(BF16)
    ffn_w1_b, ffn_w2_b = ffn_w1.astype(BF16), ffn_w2.astype(BF16)
    stream = (ctx, x, 1)
    for l in range(DEPTH):
        last = l == DEPTH - 1
        mod_l = mod[l].reshape(16, 1, 6 * D)
        pm, pa, qr, qp, kr = _projection(*stream, mod_l, norm1_g[l][None], w_in_b, l,
                                         jnp.tile(na_q_norm_g[l], NA_H)[None], jnp.tile(na_k_norm_g[l], NA_H)[None])

        ya = _rglru(pa.reshape(S, B, 2 * BW), conv_w[l], conv_b[l][None],
                    _block_diag(lru_wa[l]).astype(BF16), _block_diag(lru_wx[l]).astype(BF16),
                    lru_ba[l][:, None], lru_bx[l][:, None], lru_lambda[l][:, None])
        yb = _hgrn2(pm, hg_lb_logits, jnp.tile(hg_norm_g[l], HG_H)[None], l)
        yc = _attention(qr, qp, kr, pm, _na_bias_tables(na_rpb[l]), with_ctx=not last)

        h = _merge_ffn(*stream, mod_l, ya.reshape(S, B * BW), yb, yc, pm, w_branch_b, w_out_b, norm2_g[l][None],
                       ffn_w1_b, ffn_w2_b, l, 1 if last else 0)
        stream = (h, h, 0)
    return h
```

```python
import functools

import numpy as np
import jax
import jax.numpy as jnp
from jax import lax
from jax.experimental import pallas as pl
from jax.experimental.pallas import tpu as pltpu

F32 = jnp.float32
BF16 = jnp.bfloat16

D = 1024
DEPTH = 2
GRID_W = 64
CTX = 256
BW = 512
LRU_NB = 8
LRU_BS = BW // LRU_NB
LRU_C = 8.0
HG_H = 4
HG_DK = 128
HG_C = 64
NA_H = 8
NA_DH = 64
WIN_R = 8
WIN_C = 16
ROPE_BASE = 10000.0
MASK_VALUE = -1e30
D_FF = 4 * D
EPS = 1e-6
TM = 256
NA_RB = 4
NA_UR = 12
VMEM_LIMIT = 56 * 1024 * 1024
LOG2E = 1.4426950408889634


def _cp(sem):
    return pltpu.CompilerParams(dimension_semantics=sem, vmem_limit_bytes=VMEM_LIMIT)


def _mod_row(b, j):
    return jnp.where(j == 0, 8, b)


def _mod_kernel(c_ref, w_ref, b_ref, o_ref):
    c = c_ref[...]
    s = c * jax.nn.sigmoid(c)
    o_ref[0] = jnp.dot(s, w_ref[0], preferred_element_type=F32) + b_ref[0]


def _modulation(cond, ada_w, ada_b):
    tn = 1536
    return pl.pallas_call(
        _mod_kernel,
        grid=(DEPTH, 6 * D // tn),
        in_specs=[
            pl.BlockSpec((16, D), lambda l, n: (0, 0)),
            pl.BlockSpec((1, D, tn), lambda l, n: (l, 0, n)),
            pl.BlockSpec((1, 1, tn), lambda l, n: (l, 0, n)),
        ],
        out_specs=pl.BlockSpec((1, 16, tn), lambda l, n: (l, 0, n)),
        out_shape=jax.ShapeDtypeStruct((DEPTH, 16, 6 * D), F32),
        compiler_params=_cp(("arbitrary", "arbitrary")),
        name="modulation",
    )(cond, ada_w, ada_b.reshape(DEPTH, 1, 6 * D))


def _seg_mean_sq(x, seg):
    s = jnp.dot((x * x).astype(BF16), seg, preferred_element_type=F32)
    return s * (1.0 / NA_DH)


def _swap16(y):
    n = y.shape[-1]
    lane = lax.broadcasted_iota(jnp.int32, y.shape, 1)
    return jnp.where(lane % 32 < 16, pltpu.roll(y, n - 16, 1), pltpu.roll(y, 16, 1))


def _rope_tables(S):
    half = NA_DH // 2
    nf = half // 2
    t = np.arange(S - CTX)
    inv_freq = (ROPE_BASE ** (-np.arange(nf, dtype=np.float32) / nf)).astype(np.float32)
    ang_r = (t // GRID_W).astype(np.float32)[:, None] * inv_freq
    ang_c = (t % GRID_W).astype(np.float32)[:, None] * inv_freq
    ang = np.concatenate([ang_r, ang_r, ang_c, ang_c], axis=-1)
    sign = np.concatenate([-np.ones(nf), np.ones(nf), -np.ones(nf), np.ones(nf)]).astype(np.float32)
    cos = np.concatenate([np.ones((CTX, NA_DH), np.float32), np.cos(ang)], axis=0)
    sin = np.concatenate([np.zeros((CTX, NA_DH), np.float32), np.sin(ang) * sign], axis=0)
    return np.tile(cos, (1, NA_H)).astype(np.float32), np.tile(sin, (1, NA_H)).astype(np.float32)


NA_Q_SCALE = NA_DH ** -0.5 * LOG2E
PM_V = 5
PM_GATE0 = 6
PM_COLS = 12 * BW


def _proj_kernel(c_ref, x_ref, mod_ref, g_ref, w_ref, qg_ref, kg_ref, seg_ref, cos_ref, sin_ref,
                 p_ref, pa_ref, qr_ref, qp_ref, kr_ref):
    j = pl.program_id(1)
    x = jnp.where(j == 0, c_ref[0], x_ref[0])
    m = mod_ref[0]
    y = x * lax.rsqrt(jnp.mean(x * x, axis=-1, keepdims=True) + EPS) * g_ref[...]
    u = (y * (1.0 + m[:, D:2 * D]) + m[:, 0:D]).astype(BF16)
    dot = lambda lo, hi: jnp.dot(u, w_ref[:, lo * BW:hi * BW], preferred_element_type=F32)
    seg = seg_ref[...]
    cos = cos_ref[...]
    sin = sin_ref[...]
    q = dot(7, 8)
    qn = q * lax.rsqrt(_seg_mean_sq(q, seg) + EPS) * qg_ref[...]
    qp_ref[0] = (qn * NA_Q_SCALE).astype(BF16)
    qr_ref[0] = ((qn * cos + _swap16(qn) * sin) * NA_Q_SCALE).astype(BF16)
    k = dot(8, 9)
    kn = k * lax.rsqrt(_seg_mean_sq(k, seg) + EPS) * kg_ref[...]
    kr_ref[0] = (kn * cos + _swap16(kn) * sin).astype(BF16)
    pa_ref[...] = dot(0, 2)
    p_ref[0, :, :5 * BW] = dot(2, 7).astype(p_ref.dtype)
    p_ref[0, :, 5 * BW:] = dot(9, 16).astype(p_ref.dtype)


def _stream_specs(lat_first, first_tile=0):
    return (pl.BlockSpec((1, TM, D), lambda b, j: (b, 0, 0)),
            pl.BlockSpec((1, TM, D), lambda b, j: (b, jnp.maximum(j + first_tile - lat_first, 0), 0)))


def _projection(ctx_arr, lat_arr, lat_first, mod_l, g, w, l, q_g, k_g):
    B = lat_arr.shape[0]
    S = lat_arr.shape[1] + lat_first * TM
    nt = S // TM
    cos, sin = _rope_tables(S)
    seg = np.kron(np.eye(NA_H, dtype=np.float32), np.ones((NA_DH, NA_DH), np.float32))
    const = lambda shape: pl.BlockSpec(shape, lambda b, j: (0,) * len(shape))
    tok = lambda w_: pl.BlockSpec((1, TM, w_), lambda b, j: (b, j, 0))
    return pl.pallas_call(
        _proj_kernel,
        grid=(B, nt),
        in_specs=[
            *_stream_specs(lat_first),
            pl.BlockSpec((1, 1, 6 * D), lambda b, j: (_mod_row(b, j), 0, 0)),
            const((1, D)),
            pl.BlockSpec((None,) + w.shape[1:], lambda b, j: (l, 0, 0), pipeline_mode=pl.Buffered(1)),
            const((1, BW)), const((1, BW)), const((BW, BW)),
            pl.BlockSpec((TM, BW), lambda b, j: (j, 0)), pl.BlockSpec((TM, BW), lambda b, j: (j, 0)),
        ],
        out_specs=[tok(PM_COLS), pl.BlockSpec((TM, 2 * BW), lambda b, j: (j, b)), tok(BW), tok(BW), tok(BW)],
        out_shape=[
            jax.ShapeDtypeStruct((B, S, PM_COLS), BF16),
            jax.ShapeDtypeStruct((S, B * 2 * BW), F32),
            jax.ShapeDtypeStruct((B, S, BW), BF16),
            jax.ShapeDtypeStruct((B, S, BW), BF16),
            jax.ShapeDtypeStruct((B, S, BW), BF16),
        ],
        compiler_params=_cp(("arbitrary", "arbitrary")),
        name="projection",
    )(ctx_arr, lat_arr, mod_l, g, w, q_g, k_g, jnp.asarray(seg, BF16), jnp.asarray(cos), jnp.asarray(sin))


LRU_CW = 256
LRU_TM = 128


def _seq_tile(p, j, nt, nct):
    return jnp.where(p == 0, j, jnp.where(j < nct, nct - 1 - j, nt - 1 - (j - nct)))


def _softplus(x):
    return jnp.maximum(x, 0.0) + jnp.log1p(jnp.exp(-jnp.abs(x)))


def _gelu_tanh(x):
    return 0.5 * x * (1.0 + jnp.tanh(np.sqrt(2.0 / np.pi) * (x + 0.044715 * (x * x * x))))


def _lru_kernel(x_ref, gate_ref, prev_ref, next_ref, cw_ref, cb_ref, wa_ref, wx_ref, ba_ref, bx_ref, lam_ref,
                y_ref, hf_ref, a_ref, b_ref, carry_ref, *, nt, nct):
    TM = LRU_TM
    p = pl.program_id(1)
    j = pl.program_id(2)
    tile = _seq_tile(p, j, nt, nct)
    nb = x_ref.shape[1]

    @pl.when(j == 0)
    def _():
        carry_ref[...] = jnp.zeros_like(carry_ref)

    x = x_ref[...]
    has_prev = jnp.logical_and(tile != 0, tile != nct)
    has_next = jnp.logical_and(tile != nct - 1, tile != nt - 1)
    prev = jnp.where(has_prev, prev_ref[...], 0.0)
    nxt = jnp.where(has_next, next_ref[...], 0.0)
    xe = jnp.concatenate([prev, x, nxt], axis=0)
    cw = cw_ref[...]
    xc = cb_ref[...][None] + xe[0:TM] * cw[0:1][None] + xe[1:TM + 1] * cw[1:2][None] \
        + xe[2:TM + 2] * cw[2:3][None] + xe[3:TM + 3] * cw[3:4][None]
    xc2 = xc.reshape(TM * nb, LRU_CW)
    xb = xc2.astype(BF16)
    r = jax.nn.sigmoid(jnp.dot(xb, wa_ref[0, 0], preferred_element_type=F32) + ba_ref[0])
    i = jax.nn.sigmoid(jnp.dot(xb, wx_ref[0, 0], preferred_element_type=F32) + bx_ref[0])
    log_a = (-LRU_C) * r * _softplus(-lam_ref[0])
    a = jnp.exp(log_a)
    bc = jnp.sqrt(1.0 - a * a) * (i * xc2)
    a_ref[...] = a.reshape(TM, nb, LRU_CW)
    b_ref[...] = bc.reshape(TM, nb, LRU_CW)

    def step(t, h):
        h = a_ref[t] * h + b_ref[t]
        b_ref[t] = h
        return h

    @pl.when(p == 0)
    def _():
        carry_ref[...] = lax.fori_loop(0, TM, step, carry_ref[...], unroll=8)
        hf_ref[pl.ds(pl.multiple_of(tile * TM, TM), TM)] = b_ref[...]

    @pl.when(p == 1)
    def _():
        carry_ref[...] = lax.fori_loop(0, TM, lambda s, h: step(TM - 1 - s, h), carry_ref[...], unroll=8)
        hf = hf_ref[pl.ds(pl.multiple_of(tile * TM, TM), TM)]
        y_ref[...] = (hf + b_ref[...]) * _gelu_tanh(gate_ref[...])


def _rglru(pa, conv_w, conv_b, wa_bd, wx_bd, ba, bx, lam):
    S, B, _ = pa.shape
    TM = LRU_TM
    nt = S // TM
    nct = CTX // TM
    ncw = BW // LRU_CW
    tile = functools.partial(_seq_tile, nt=nt, nct=nct)
    vec = lambda: pl.BlockSpec((1, 1, LRU_CW), lambda c, p, j: (p, 0, c))
    return pl.pallas_call(
        functools.partial(_lru_kernel, nt=nt, nct=nct),
        grid=(ncw, 2, nt),
        in_specs=[
            pl.BlockSpec((TM, B, LRU_CW), lambda c, p, j: (tile(p, j), 0, c)),
            pl.BlockSpec((TM, B, LRU_CW), lambda c, p, j: (tile(p, j), 0, ncw + c)),
            pl.BlockSpec((2, B, LRU_CW), lambda c, p, j: (jnp.maximum(tile(p, j) * (TM // 2) - 1, 0), 0, c)),
            pl.BlockSpec((1, B, LRU_CW), lambda c, p, j: (jnp.minimum((tile(p, j) + 1) * TM, S - 1), 0, c)),
            pl.BlockSpec((4, LRU_CW), lambda c, p, j: (0, c)),
            pl.BlockSpec((1, LRU_CW), lambda c, p, j: (0, c)),
            pl.BlockSpec((1, 1, LRU_CW, LRU_CW), lambda c, p, j: (p, c, 0, 0)),
            pl.BlockSpec((1, 1, LRU_CW, LRU_CW), lambda c, p, j: (p, c, 0, 0)),
            vec(), vec(), vec(),
        ],
        out_specs=pl.BlockSpec((TM, B, LRU_CW), lambda c, p, j: (jnp.where(p == 0, nct - 1, tile(p, j)), 0, c)),
        out_shape=jax.ShapeDtypeStruct((S, B, BW), F32),
        scratch_shapes=[
            pltpu.VMEM((S, B, LRU_CW), F32),
            pltpu.VMEM((TM, B, LRU_CW), F32),
            pltpu.VMEM((TM, B, LRU_CW), F32),
            pltpu.VMEM((B, LRU_CW), F32),
        ],
        compiler_params=_cp(("arbitrary", "arbitrary", "arbitrary")),
        name="rglru",
    )(pa, pa, pa, pa, conv_w, conv_b, wa_bd, wx_bd, ba, bx, lam)


def _block_diag(w):
    per = LRU_CW // LRU_BS
    w = w.reshape(2, BW // LRU_CW, per, LRU_BS, LRU_BS)
    eye = jnp.eye(per, dtype=w.dtype)
    out = jnp.einsum('dcnio,nm->dcnimo', w, eye)
    return out.reshape(2, BW // LRU_CW, LRU_CW, LRU_CW)


HG_NLEV = 6
HG_FINE = 3
HG_ROWS = (HG_NLEV + 2) * HG_C


def _hgrn_constants():
    C = HG_C
    W = np.zeros((2, HG_NLEV + 2, C, C), np.float32)
    M = np.zeros((2, HG_NLEV + 1, C, C), np.float32)
    for lv in range(HG_NLEV):
        m = C >> (lv + 1)
        for t in range(C):
            blk = t // m
            if blk % 2 == 1:
                p = blk * m
                W[0, lv, t, p:t + 1] = 1.0
                W[1, lv, t, p:t] = 1.0
            else:
                p = (blk + 1) * m
                W[0, lv, t, t + 1:p] = 1.0
                W[1, lv, t, t:p] = 1.0
        for t in range(C):
            for s in range(C):
                if (t // m) % 2 == 1 and s // m == t // m - 1:
                    M[0, lv, t, s] = 1.0
                    M[1, lv, s, t] = 1.0
    for t in range(C):
        W[0, HG_NLEV, t, :t + 1] = 1.0
        W[1, HG_NLEV, t, t:] = 1.0
        W[0, HG_NLEV + 1, t, t + 1:] = 1.0
        W[1, HG_NLEV + 1, t, :t] = 1.0
        M[:, HG_NLEV, t, t] = 1.0
    W = np.concatenate([W[:, HG_NLEV:HG_NLEV + 1], W[:, HG_NLEV - HG_FINE:HG_NLEV]], axis=1)
    W = W.reshape(2, (1 + HG_FINE) * C, C)
    lev = np.full((2, C, C), -1, np.int32)
    for lv in range(HG_NLEV + 1):
        lev[M[:, lv] > 0.5] = lv
    return np.concatenate([W, W], axis=-1), lev


def _hgrn_kernel(q_ref, f_ref, v_ref, og_ref, lbl_ref, ng_ref, w_ref, m_ref, y_ref, of_ref, st_ref, *, nt, layer):
    p = pl.program_id(1)
    j = pl.program_id(2)
    tile = _seq_tile(p, j, nt, CTX // TM)
    C = HG_C
    nchunk = TM // C

    @pl.when(j == 0)
    def _():
        st_ref[...] = jnp.zeros_like(st_ref)

    logits = lbl_ref[0]
    e = jnp.exp(logits - jnp.max(logits, axis=0, keepdims=True))
    soft = e / jnp.sum(e, axis=0, keepdims=True)
    lb = jnp.sum(soft[0:layer + 1], axis=0, keepdims=True) - soft[0:1]

    wmat = w_ref[0]
    lev = m_ref[0]
    row0 = pl.multiple_of(tile * TM, TM)
    inter = HG_NLEV * C

    def direction(backward):
        f = lb + (1.0 - lb) * jax.nn.sigmoid(f_ref[0].astype(F32))
        g = jnp.log(f) * LOG2E
        k = (1.0 - f).astype(BF16)
        qr = q_ref[0].astype(F32)
        q = (qr * jax.nn.sigmoid(qr)).astype(BF16)
        v = v_ref[0]
        g1 = g.astype(BF16)
        g2 = (g - g1.astype(F32)).astype(BF16)
        g_split = jnp.concatenate(
            [jnp.concatenate([g1[c * C:(c + 1) * C], g2[c * C:(c + 1) * C]], axis=0) for c in range(nchunk)], axis=1)
        sums = jnp.dot(wmat, g_split, preferred_element_type=F32)
        run = sums[:C]
        coarse = []
        for lv in range(HG_NLEV - HG_FINE):
            m = C >> (lv + 1)
            blocks = []
            for kb in range(C // (2 * m)):
                edge = (2 * kb + 1) * m - (0 if backward else 1)
                ref = run[edge:edge + 1]
                early, late = run[2 * kb * m:(2 * kb + 1) * m], run[(2 * kb + 1) * m:(2 * kb + 2) * m]
                blocks += [early - ref, ref - late] if backward else [ref - early, late - ref]
            coarse.append(jnp.concatenate(blocks, axis=0))
        to_edge = (run[0:1] if backward else run[C - 1:C]) - run
        ex32 = jnp.exp2(jnp.concatenate(coarse + [sums[C:], run, to_edge], axis=0))
        ex = ex32.astype(BF16)
        dec_row = inter if backward else inter + C - 1
        dec_all = ex32[dec_row:dec_row + 1]
        intra, q_in, upd = {}, {}, {}
        for c in range(nchunk):
            rows = slice(c * C, (c + 1) * C)
            for h in range(HG_H):
                hs = slice(h * HG_DK, (h + 1) * HG_DK)
                qh, kh, vh = q[rows, hs], k[rows, hs], v[rows, hs]
                eh = ex[:, c * BW + h * HG_DK:c * BW + (h + 1) * HG_DK].reshape(HG_NLEV + 2, C, HG_DK)
                qs = jnp.concatenate([qh[None] * eh[:HG_NLEV], qh[None]], axis=0)
                ks = jnp.concatenate([kh[None] * eh[:HG_NLEV], kh[None]], axis=0)
                att_l = jnp.einsum('lqd,lkd->lqk', qs, ks, preferred_element_type=F32)
                att = jnp.zeros((C, C), F32)
                for lv in range(HG_NLEV + 1):
                    att = jnp.where(lev == lv, att_l[lv], att)
                intra[c, h] = jnp.dot(att.astype(BF16), vh, preferred_element_type=F32)
                q_in[c, h] = qh * eh[HG_NLEV]
                upd[c, h] = lax.dot_general(vh, kh * eh[HG_NLEV + 1], (((0,), (0,)), ((), ())),
                                            preferred_element_type=F32)
        st = [st_ref[h] for h in range(HG_H)]
        outs = [None] * nchunk
        for c in (reversed(range(nchunk)) if backward else range(nchunk)):
            row = []
            for h in range(HG_H):
                row.append(intra[c, h] + lax.dot_general(q_in[c, h], st[h].astype(BF16), (((1,), (1,)), ((), ())),
                                                         preferred_element_type=F32))
                st[h] = dec_all[:, c * BW + h * HG_DK:c * BW + (h + 1) * HG_DK] * st[h] + upd[c, h]
            outs[c] = jnp.concatenate(row, axis=-1)
        for h in range(HG_H):
            st_ref[h] = st[h]
        return jnp.concatenate(outs, axis=0)

    @pl.when(p == 0)
    def _():
        of_ref[pl.ds(row0, TM), :] = direction(False)

    @pl.when(p == 1)
    def _():
        o = direction(True) + of_ref[pl.ds(row0, TM), :]
        ys = []
        for h in range(HG_H):
            oh = o[:, h * HG_DK:(h + 1) * HG_DK]
            ys.append(oh * lax.rsqrt(jnp.mean(oh * oh, axis=-1, keepdims=True) + EPS))
        yn = jnp.concatenate(ys, axis=-1) * ng_ref[...]
        y_ref[0] = (yn * jax.nn.sigmoid(og_ref[0].astype(F32))).astype(y_ref.dtype)


def _hgrn2(pm, lb_logits, norm_g, layer):
    B, S, _ = pm.shape
    nt = S // TM
    tile = functools.partial(_seq_tile, nt=nt, nct=CTX // TM)
    w_np, m_np = _hgrn_constants()
    w_c = jnp.asarray(w_np, BF16)
    m_c = jnp.asarray(m_np, jnp.int32)
    col = lambda cb: pl.BlockSpec((1, TM, BW), lambda b, p, j: (b, tile(p, j), cb))
    return pl.pallas_call(
        functools.partial(_hgrn_kernel, nt=nt, layer=layer),
        grid=(B, 2, nt),
        in_specs=[
            col(0),
            pl.BlockSpec((1, TM, BW), lambda b, p, j: (b, tile(p, j), 1 + p)),
            col(3), col(4),
            pl.BlockSpec((1, DEPTH, BW), lambda b, p, j: (p, 0, 0)),
            pl.BlockSpec((1, BW), lambda b, p, j: (0, 0)),
            pl.BlockSpec((1, (1 + HG_FINE) * HG_C, 2 * HG_C), lambda b, p, j: (p, 0, 0)),
            pl.BlockSpec((1, HG_C, HG_C), lambda b, p, j: (p, 0, 0)),
        ],
        out_specs=pl.BlockSpec((1, TM, BW), lambda b, p, j: (b, jnp.where(p == 0, CTX // TM - 1, tile(p, j)), 0)),
        out_shape=jax.ShapeDtypeStruct((B, S, BW), BF16),
        scratch_shapes=[
            pltpu.VMEM((S, BW), F32),
            pltpu.VMEM((HG_H, HG_DK, HG_DK), F32),
        ],
        compiler_params=_cp(("arbitrary", "arbitrary", "arbitrary")),
        name="hgrn2",
    )(pm, pm, pm, pm, lb_logits, norm_g, w_c, m_c)


NA_GH = 4
NA_GW = NA_GH * NA_DH


def _head_mask(h, n_rows):
    lane = lax.broadcasted_iota(jnp.int32, (n_rows, NA_GW), 1)
    return lane // NA_DH == h


def _softmax_pv(s_parts, v_parts):
    out = jnp.zeros((TM, NA_GW), F32)
    for h in range(NA_GH):
        parts = [s[h * TM:(h + 1) * TM] for s in s_parts]
        m = parts[0].max(axis=-1, keepdims=True)
        for s in parts[1:]:
            m = jnp.maximum(m, s.max(axis=-1, keepdims=True))
        ps = [jnp.exp2(s - m) for s in parts]
        den = ps[0].sum(axis=-1, keepdims=True)
        for pp in ps[1:]:
            den = den + pp.sum(axis=-1, keepdims=True)
        acc = jnp.zeros((TM, NA_GW), F32)
        for pp, vv in zip(ps, v_parts):
            acc = acc + jnp.dot(pp.astype(BF16), vv, preferred_element_type=F32)
        out = out + jnp.where(_head_mask(h, TM), acc * (1.0 / den), 0.0)
    return out.astype(BF16)


def _na_kernel(qr_ref, qp_ref, k0_ref, k1_ref, k2_ref, v0_ref, v1_ref, v2_ref, kc_ref, vc_ref, bias_ref, y_ref,
               *, with_ctx):
    step = pl.program_id(0)
    masks_q = [_head_mask(h, TM) for h in range(NA_GH)]
    rows = 2048 // GRID_W

    def stack_heads(q):
        return jnp.concatenate([jnp.where(masks_q[h], q, jnp.zeros_like(q)) for h in range(NA_GH)], axis=0)

    nt_dims = (((1,), (1,)), ((), ()))

    def bias_tile_index(iq, key_row):
        lat = step - 1 if with_ctx else step
        r = lat * NA_RB + iq
        u0 = _na_key_base(lat) * NA_RB
        d0 = jnp.clip(r - WIN_R // 2, 0, rows - WIN_R) - r + (WIN_R - 1)
        d = u0 + key_row - r + (WIN_R - 1)
        both = jnp.logical_and(d >= d0, d + 1 <= d0 + WIN_R - 1)
        variant = jnp.where(both, 0, jnp.where(d == d0 + WIN_R - 1, 1, jnp.where(d == d0 - 1, 2, 3)))
        return variant, jnp.clip(d + 1, 0, 2 * WIN_R - 1)

    def part_bias(g, i):
        tiles = {}
        for iq in range(NA_RB):
            for jp in range(TM // (2 * GRID_W)):
                tiles[iq, jp] = bias_tile_index(iq, i * NA_RB + 2 * jp)
        blocks = []
        for h in range(NA_GH):
            for iq in range(NA_RB):
                blocks.append(jnp.concatenate(
                    [bias_ref[tiles[iq, jp][0], g * NA_GH + h, tiles[iq, jp][1]]
                     for jp in range(TM // (2 * GRID_W))], axis=1))
        return jnp.concatenate(blocks, axis=0)

    def latent():
        for g in range(NA_H // NA_GH):
            gs = slice(g * NA_GW, (g + 1) * NA_GW)
            q_rot = stack_heads(qr_ref[0, :, gs])
            s_parts, v_parts = [], []
            for i, (k_ref, v_ref) in enumerate(((k0_ref, v0_ref), (k1_ref, v1_ref), (k2_ref, v2_ref))):
                s = lax.dot_general(q_rot, k_ref[0, :, gs], nt_dims, preferred_element_type=F32)
                s_parts.append(s + part_bias(g, i))
                v_parts.append(v_ref[0, :, gs])
            s_parts.append(lax.dot_general(stack_heads(qp_ref[0, :, gs]), kc_ref[0, :, gs], nt_dims,
                                           preferred_element_type=F32))
            v_parts.append(vc_ref[0, :, gs])
            y_ref[0, :, gs] = _softmax_pv(s_parts, v_parts)

    def context():
        for g in range(NA_H // NA_GH):
            gs = slice(g * NA_GW, (g + 1) * NA_GW)
            s_ctx = lax.dot_general(stack_heads(qp_ref[0, :, gs]), kc_ref[0, :, gs], nt_dims,
                                    preferred_element_type=F32)
            y_ref[0, :, gs] = _softmax_pv([s_ctx], [vc_ref[0, :, gs]])

    if with_ctx:
        pl.when(step == 0)(context)
        pl.when(step > 0)(latent)
    else:
        latent()


def _na_bias_tiles(rpb):
    col = np.arange(GRID_W)
    c_start = np.clip(col - WIN_C // 2, 0, GRID_W - WIN_C)
    col_ok = (col[None, :] >= c_start[:, None]) & (col[None, :] < c_start[:, None] + WIN_C)
    d_col = np.clip(col[None, :] - col[:, None], -(WIN_C - 1), WIN_C - 1) + (WIN_C - 1)
    hot = jnp.asarray(np.eye(2 * WIN_C - 1, dtype=np.float32)[d_col])
    by_col = jnp.einsum('hrc,qkc->hrqk', rpb.astype(F32), hot, precision=lax.Precision.HIGHEST) * LOG2E
    by_col = jnp.where(jnp.asarray(col_ok)[None, None], by_col, MASK_VALUE)
    masked = jnp.full((NA_H, 1, GRID_W, GRID_W), MASK_VALUE, F32)
    padded = jnp.concatenate([masked, by_col, masked], axis=1)
    first, second = padded[:, :-1], padded[:, 1:]
    none = jnp.broadcast_to(masked, first.shape)
    pair = lambda a, b: jnp.concatenate([a, b], axis=-1)
    return jnp.stack([pair(first, second), pair(first, none), pair(none, second), pair(none, none)])


def _na_key_base(step_latent):
    return jnp.clip(step_latent - 1, 0, 2048 // TM - NA_UR * GRID_W // TM)


def _attention(qr, qp, kr, pm, bias, with_ctx):
    B, S, _ = qr.shape
    nlat = (S - CTX) // TM
    off = 0 if with_ctx else 1

    def qtile(t, b):
        return (b, t + off, 0)

    def ktile(i, cb=0):
        def f(t, b):
            return (b, 1 + _na_key_base(t + off - 1) + i, cb)
        return f

    blk = lambda f: pl.BlockSpec((1, TM, BW), f)
    return pl.pallas_call(
        functools.partial(_na_kernel, with_ctx=with_ctx),
        grid=(nlat + (1 if with_ctx else 0), B),
        in_specs=[blk(qtile), blk(qtile), blk(ktile(0)), blk(ktile(1)), blk(ktile(2)),
                  blk(ktile(0, PM_V)), blk(ktile(1, PM_V)), blk(ktile(2, PM_V)),
                  blk(lambda t, b: (b, 0, 0)), blk(lambda t, b: (b, 0, PM_V)),
                  pl.BlockSpec(bias.shape, lambda t, b: (0,) * bias.ndim)],
        out_specs=blk(lambda t, b: (b, t, 0)),
        out_shape=jax.ShapeDtypeStruct((B, S - off * TM, BW), BF16),
        compiler_params=_cp(("arbitrary", "arbitrary")),
        name="attention",
    )(qr, qp, kr, kr, kr, pm, pm, pm, kr, pm, bias)


def _merge_ffn_kernel(c_ref, h_ref, mod_ref, ya_ref, yb_ref, yc_ref, g0_ref, g1_ref, g2_ref, wb_ref, wo_ref, ng_ref,
                      w1_ref, w2_ref, o_ref, *, first_tile):
    h_in = h_ref[0]
    if first_tile == 0:
        h_in = jnp.where(pl.program_id(1) == 0, c_ref[0], h_in)
    m = mod_ref[0]
    acc = None
    for n, (y_ref, g_ref) in enumerate(((ya_ref, g0_ref), (yb_ref, g1_ref), (yc_ref, g2_ref))):
        y = y_ref[...].reshape(TM, BW).astype(BF16)
        term = jax.nn.sigmoid(g_ref[0].astype(F32)) * jnp.dot(y, wb_ref[n], preferred_element_type=F32)
        acc = term if acc is None else acc + term
    mixed = jnp.dot(acc.astype(BF16), wo_ref[...], preferred_element_type=F32)
    x = h_in + m[:, 2 * D:3 * D] * mixed
    y = x * lax.rsqrt(jnp.mean(x * x, axis=-1, keepdims=True) + EPS) * ng_ref[...]
    u = (y * (1.0 + m[:, 4 * D:5 * D]) + m[:, 3 * D:4 * D]).astype(BF16)
    a = jnp.maximum(jnp.dot(u, w1_ref[...], preferred_element_type=F32), 0.0)
    out = jnp.dot((a * a).astype(BF16), w2_ref[...], preferred_element_type=F32)
    o_ref[0] = x + m[:, 5 * D:6 * D] * out


def _merge_ffn(ctx_arr, lat_arr, lat_first, mod_l, ya, yb, yc, pm, w_branch, w_out, norm_g, w1, w2, l, first_tile):
    B, S, _ = yb.shape
    nt = S // TM - first_tile
    tok = lambda w: pl.BlockSpec((1, TM, w), lambda b, j: (b, j + first_tile, 0))
    gate = lambda n: pl.BlockSpec((1, TM, D), lambda b, j: (b, j + first_tile, PM_GATE0 * BW // D + n))
    layer = lambda shape: pl.BlockSpec((None,) + shape, lambda b, j: (l,) + (0,) * len(shape),
                                       pipeline_mode=pl.Buffered(1))
    return pl.pallas_call(
        functools.partial(_merge_ffn_kernel, first_tile=first_tile),
        grid=(B, nt),
        in_specs=[
            *_stream_specs(lat_first, first_tile),
            pl.BlockSpec((1, 1, 6 * D), lambda b, j: (_mod_row(b, j + first_tile), 0, 0)),
            pl.BlockSpec((TM, BW), lambda b, j: (j + first_tile, b)),
            tok(BW),
            pl.BlockSpec((1, TM, BW), lambda b, j: (b, j, 0)),
            gate(0), gate(1), gate(2),
            layer((3, BW, D)), layer((D, D)), pl.BlockSpec((1, D), lambda b, j: (0, 0)),
            layer((D, D_FF)), layer((D_FF, D)),
        ],
        out_specs=pl.BlockSpec((1, TM, D), lambda b, j: (b, j, 0)),
        out_shape=jax.ShapeDtypeStruct((B, nt * TM, D), F32),
        compiler_params=_cp(("arbitrary", "arbitrary")),
        name="merge_ffn",
    )(ctx_arr, lat_arr, mod_l, ya, yb, yc, pm, pm, pm, w_branch, w_out, norm_g, w1, w2)


def kernel(x, c, ctx, c_ctx, ada_w, ada_b, norm1_g, norm2_g, w_in, conv_w, conv_b, lru_wa, lru_ba, lru_wx, lru_bx,
           lru_lambda, hg_lb_logits, hg_norm_g, na_q_norm_g, na_k_norm_g, na_rpb, w_branch, w_out, ffn_w1, ffn_w2):
    B, T, _ = x.shape
    S = CTX + T
    cond = jnp.zeros((16, D), F32).at[:B].set(c).at[8].set(c_ctx)
    mod = _modulation(cond, ada_w, ada_b)
    w_in_b, w_branch_b, w_out_b = w_in.astype(BF16), w_branch.astype(BF16), w_out.astype(BF16)
    ffn_w1_b, ffn_w2_b = ffn_w1.astype(BF16), ffn_w2.astype(BF16)
    stream = (ctx, x, 1)
    for l in range(DEPTH):
        last = l == DEPTH - 1
        mod_l = mod[l].reshape(16, 1, 6 * D)
        pm, pa, qr, qp, kr = _projection(*stream, mod_l, norm1_g[l][None], w_in_b, l,
                                         jnp.tile(na_q_norm_g[l], NA_H)[None], jnp.tile(na_k_norm_g[l], NA_H)[None])

        ya = _rglru(pa.reshape(S, B, 2 * BW), conv_w[l], conv_b[l][None],
                    _block_diag(lru_wa[l]).astype(BF16), _block_diag(lru_wx[l]).astype(BF16),
                    lru_ba[l][:, None], lru_bx[l][:, None], lru_lambda[l][:, None])
        yb = _hgrn2(pm, hg_lb_logits, jnp.tile(hg_norm_g[l], HG_H)[None], l)
        yc = _attention(qr, qp, kr, pm, _na_bias_tiles(na_rpb[l]), with_ctx=not last)

        h = _merge_ffn(*stream, mod_l, ya.reshape(S, B * BW), yb, yc, pm, w_branch_b, w_out_b, norm2_g[l][None],
                       ffn_w1_b, ffn_w2_b, l, 1 if last else 0)
        stream = (h, h, 0)
    return h
```

```python
import functools

import numpy as np
import jax
import jax.numpy as jnp
from jax import lax
from jax.experimental import pallas as pl
from jax.experimental.pallas import tpu as pltpu

F32 = jnp.float32
BF16 = jnp.bfloat16

D = 1024
DEPTH = 2
GRID_W = 64
CTX = 256
BW = 512
LRU_NB = 8
LRU_BS = BW // LRU_NB
LRU_C = 8.0
HG_H = 4
HG_DK = 128
HG_C = 64
NA_H = 8
NA_DH = 64
WIN_R = 8
WIN_C = 16
ROPE_BASE = 10000.0
MASK_VALUE = -1e30
D_FF = 4 * D
EPS = 1e-6
TM = 256
NA_RB = 4
NA_UR = 12
VMEM_LIMIT = 56 * 1024 * 1024
LOG2E = 1.4426950408889634


def _cp(sem):
    return pltpu.CompilerParams(dimension_semantics=sem, vmem_limit_bytes=VMEM_LIMIT)


def _mod_row(b, j):
    return jnp.where(j == 0, 8, b)


def _mod_kernel(c_ref, w_ref, b_ref, o_ref):
    c = c_ref[...]
    s = c * jax.nn.sigmoid(c)
    o_ref[0] = jnp.dot(s, w_ref[0], preferred_element_type=F32) + b_ref[0]


def _modulation(cond, ada_w, ada_b):
    tn = 1536
    return pl.pallas_call(
        _mod_kernel,
        grid=(DEPTH, 6 * D // tn),
        in_specs=[
            pl.BlockSpec((16, D), lambda l, n: (0, 0)),
            pl.BlockSpec((1, D, tn), lambda l, n: (l, 0, n)),
            pl.BlockSpec((1, 1, tn), lambda l, n: (l, 0, n)),
        ],
        out_specs=pl.BlockSpec((1, 16, tn), lambda l, n: (l, 0, n)),
        out_shape=jax.ShapeDtypeStruct((DEPTH, 16, 6 * D), F32),
        compiler_params=_cp(("arbitrary", "arbitrary")),
        name="modulation",
    )(cond, ada_w, ada_b.reshape(DEPTH, 1, 6 * D))


def _seg_mean_sq(x, seg):
    s = jnp.dot((x * x).astype(BF16), seg, preferred_element_type=F32)
    return s * (1.0 / NA_DH)


def _swap16(y):
    n = y.shape[-1]
    lane = lax.broadcasted_iota(jnp.int32, y.shape, 1)
    return jnp.where(lane % 32 < 16, pltpu.roll(y, n - 16, 1), pltpu.roll(y, 16, 1))


def _rope_tables(S):
    half = NA_DH // 2
    nf = half // 2
    t = np.arange(S - CTX)
    inv_freq = (ROPE_BASE ** (-np.arange(nf, dtype=np.float32) / nf)).astype(np.float32)
    ang_r = (t // GRID_W).astype(np.float32)[:, None] * inv_freq
    ang_c = (t % GRID_W).astype(np.float32)[:, None] * inv_freq
    ang = np.concatenate([ang_r, ang_r, ang_c, ang_c], axis=-1)
    sign = np.concatenate([-np.ones(nf), np.ones(nf), -np.ones(nf), np.ones(nf)]).astype(np.float32)
    cos = np.concatenate([np.ones((CTX, NA_DH), np.float32), np.cos(ang)], axis=0)
    sin = np.concatenate([np.zeros((CTX, NA_DH), np.float32), np.sin(ang) * sign], axis=0)
    return np.tile(cos, (1, NA_H)).astype(np.float32), np.tile(sin, (1, NA_H)).astype(np.float32)


NA_Q_SCALE = NA_DH ** -0.5 * LOG2E
PM_V = 5
PM_GATE0 = 6
PM_COLS = 12 * BW


def _proj_kernel(c_ref, x_ref, mod_ref, g_ref, w_ref, qg_ref, kg_ref, seg_ref, cos_ref, sin_ref,
                 p_ref, pa_ref, qr_ref, qp_ref, kr_ref):
    j = pl.program_id(0)
    b = pl.program_id(1)
    x = jnp.where(j == 0, c_ref[0], x_ref[0])
    m = mod_ref[0]
    y = x * lax.rsqrt(jnp.mean(x * x, axis=-1, keepdims=True) + EPS) * g_ref[...]
    u = (y * (1.0 + m[:, D:2 * D]) + m[:, 0:D]).astype(BF16)
    dot = lambda lo, hi: jnp.dot(u, w_ref[:, lo * BW:hi * BW], preferred_element_type=F32)
    seg = seg_ref[...]
    cos = cos_ref[...]
    sin = sin_ref[...]
    q = dot(7, 8)
    qn = q * lax.rsqrt(_seg_mean_sq(q, seg) + EPS) * qg_ref[...]
    qp_ref[0] = (qn * NA_Q_SCALE).astype(BF16)
    qr_ref[0] = ((qn * cos + _swap16(qn) * sin) * NA_Q_SCALE).astype(BF16)
    k = dot(8, 9)
    kn = k * lax.rsqrt(_seg_mean_sq(k, seg) + EPS) * kg_ref[...]
    kr_ref[0] = (kn * cos + _swap16(kn) * sin).astype(BF16)
    pa_ref[:, pl.ds(b, 1), :] = dot(0, 2)[:, None, :]
    p_ref[0, :, :5 * BW] = dot(2, 7).astype(p_ref.dtype)
    p_ref[0, :, 5 * BW:] = dot(9, 16).astype(p_ref.dtype)


def _stream_specs(lat_first, first_tile=0):
    return (pl.BlockSpec((1, TM, D), lambda j, b: (jnp.where(j + first_tile == 0, b, 0), 0, 0)),
            pl.BlockSpec((1, TM, D), lambda j, b: (b, jnp.maximum(j + first_tile - lat_first, 0), 0)))


def _projection(ctx_arr, lat_arr, lat_first, mod_l, g, w, l, q_g, k_g):
    B = lat_arr.shape[0]
    S = lat_arr.shape[1] + lat_first * TM
    nt = S // TM
    cos, sin = _rope_tables(S)
    seg = np.kron(np.eye(NA_H, dtype=np.float32), np.ones((NA_DH, NA_DH), np.float32))
    const = lambda shape: pl.BlockSpec(shape, lambda j, b: (0,) * len(shape))
    tok = lambda w_: pl.BlockSpec((1, TM, w_), lambda j, b: (b, j, 0))
    return pl.pallas_call(
        _proj_kernel,
        grid=(nt, B),
        in_specs=[
            *_stream_specs(lat_first),
            pl.BlockSpec((1, 1, 6 * D), lambda j, b: (_mod_row(b, j), 0, 0)),
            const((1, D)),
            pl.BlockSpec((None,) + w.shape[1:], lambda j, b: (l, 0, 0), pipeline_mode=pl.Buffered(1)),
            const((1, BW)), const((1, BW)), const((BW, BW)),
            pl.BlockSpec((TM, BW), lambda j, b: (j, 0)), pl.BlockSpec((TM, BW), lambda j, b: (j, 0)),
        ],
        out_specs=[tok(PM_COLS), pl.BlockSpec((TM, B, 2 * BW), lambda j, b: (j, 0, 0)), tok(BW), tok(BW), tok(BW)],
        out_shape=[
            jax.ShapeDtypeStruct((B, S, PM_COLS), BF16),
            jax.ShapeDtypeStruct((S, B, 2 * BW), F32),
            jax.ShapeDtypeStruct((B, S, BW), BF16),
            jax.ShapeDtypeStruct((B, S, BW), BF16),
            jax.ShapeDtypeStruct((B, S, BW), BF16),
        ],
        compiler_params=_cp(("arbitrary", "arbitrary")),
        name="projection",
    )(ctx_arr, lat_arr, mod_l, g, w, q_g, k_g, jnp.asarray(seg, BF16), jnp.asarray(cos), jnp.asarray(sin))


LRU_CW = 256
LRU_TM = 128


def _seq_tile(p, j, nt, nct):
    return jnp.where(p == 0, j, jnp.where(j < nct, nct - 1 - j, nt - 1 - (j - nct)))


def _softplus(x):
    return jnp.maximum(x, 0.0) + jnp.log1p(jnp.exp(-jnp.abs(x)))


def _gelu_tanh(x):
    return 0.5 * x * (1.0 + jnp.tanh(np.sqrt(2.0 / np.pi) * (x + 0.044715 * (x * x * x))))


def _lru_kernel(x_ref, gate_ref, prev_ref, next_ref, cw_ref, cb_ref, wa_ref, wx_ref, ba_ref, bx_ref, lam_ref,
                y_ref, hf_ref, a_ref, b_ref, carry_ref, *, nt, nct):
    TM = LRU_TM
    p = pl.program_id(1)
    j = pl.program_id(2)
    tile = _seq_tile(p, j, nt, nct)
    nb = x_ref.shape[1]

    @pl.when(j == 0)
    def _():
        carry_ref[...] = jnp.zeros_like(carry_ref)

    x = x_ref[...]
    has_prev = jnp.logical_and(tile != 0, tile != nct)
    has_next = jnp.logical_and(tile != nct - 1, tile != nt - 1)
    prev = jnp.where(has_prev, prev_ref[...], 0.0)
    nxt = jnp.where(has_next, next_ref[...], 0.0)
    xe = jnp.concatenate([prev, x, nxt], axis=0)
    cw = cw_ref[...]
    xc = cb_ref[...][None] + xe[0:TM] * cw[0:1][None] + xe[1:TM + 1] * cw[1:2][None] \
        + xe[2:TM + 2] * cw[2:3][None] + xe[3:TM + 3] * cw[3:4][None]
    xc2 = xc.reshape(TM * nb, LRU_CW)
    xb = xc2.astype(BF16)
    r = jax.nn.sigmoid(jnp.dot(xb, wa_ref[0, 0], preferred_element_type=F32) + ba_ref[0])
    i = jax.nn.sigmoid(jnp.dot(xb, wx_ref[0, 0], preferred_element_type=F32) + bx_ref[0])
    log_a = (-LRU_C) * r * _softplus(-lam_ref[0])
    a = jnp.exp(log_a)
    bc = jnp.sqrt(1.0 - a * a) * (i * xc2)
    a_ref[...] = a.reshape(TM, nb, LRU_CW)
    b_ref[...] = bc.reshape(TM, nb, LRU_CW)

    def step(t, h):
        h = a_ref[t] * h + b_ref[t]
        b_ref[t] = h
        return h

    @pl.when(p == 0)
    def _():
        carry_ref[...] = lax.fori_loop(0, TM, step, carry_ref[...], unroll=8)
        hf_ref[pl.ds(pl.multiple_of(tile * TM, TM), TM)] = b_ref[...]

    @pl.when(p == 1)
    def _():
        carry_ref[...] = lax.fori_loop(0, TM, lambda s, h: step(TM - 1 - s, h), carry_ref[...], unroll=8)
        hf = hf_ref[pl.ds(pl.multiple_of(tile * TM, TM), TM)]
        y_ref[...] = (hf + b_ref[...]) * _gelu_tanh(gate_ref[...])


def _rglru(pa, conv_w, conv_b, wa_bd, wx_bd, ba, bx, lam):
    S, B, _ = pa.shape
    TM = LRU_TM
    nt = S // TM
    nct = CTX // TM
    ncw = BW // LRU_CW
    tile = functools.partial(_seq_tile, nt=nt, nct=nct)
    vec = lambda: pl.BlockSpec((1, 1, LRU_CW), lambda c, p, j: (p, 0, c))
    return pl.pallas_call(
        functools.partial(_lru_kernel, nt=nt, nct=nct),
        grid=(ncw, 2, nt),
        in_specs=[
            pl.BlockSpec((TM, B, LRU_CW), lambda c, p, j: (tile(p, j), 0, c)),
            pl.BlockSpec((TM, B, LRU_CW), lambda c, p, j: (tile(p, j), 0, ncw + c)),
            pl.BlockSpec((2, B, LRU_CW), lambda c, p, j: (jnp.maximum(tile(p, j) * (TM // 2) - 1, 0), 0, c)),
            pl.BlockSpec((1, B, LRU_CW), lambda c, p, j: (jnp.minimum((tile(p, j) + 1) * TM, S - 1), 0, c)),
            pl.BlockSpec((4, LRU_CW), lambda c, p, j: (0, c)),
            pl.BlockSpec((1, LRU_CW), lambda c, p, j: (0, c)),
            pl.BlockSpec((1, 1, LRU_CW, LRU_CW), lambda c, p, j: (p, c, 0, 0)),
            pl.BlockSpec((1, 1, LRU_CW, LRU_CW), lambda c, p, j: (p, c, 0, 0)),
            vec(), vec(), vec(),
        ],
        out_specs=pl.BlockSpec((TM, B, LRU_CW), lambda c, p, j: (jnp.where(p == 0, nct - 1, tile(p, j)), 0, c)),
        out_shape=jax.ShapeDtypeStruct((S, B, BW), F32),
        scratch_shapes=[
            pltpu.VMEM((S, B, LRU_CW), F32),
            pltpu.VMEM((TM, B, LRU_CW), F32),
            pltpu.VMEM((TM, B, LRU_CW), F32),
            pltpu.VMEM((B, LRU_CW), F32),
        ],
        compiler_params=_cp(("arbitrary", "arbitrary", "arbitrary")),
        name="rglru",
    )(pa, pa, pa, pa, conv_w, conv_b, wa_bd, wx_bd, ba, bx, lam)


def _block_diag(w):
    per = LRU_CW // LRU_BS
    w = w.reshape(2, BW // LRU_CW, per, LRU_BS, LRU_BS)
    eye = jnp.eye(per, dtype=w.dtype)
    out = jnp.einsum('dcnio,nm->dcnimo', w, eye)
    return out.reshape(2, BW // LRU_CW, LRU_CW, LRU_CW)


HG_NLEV = 6
HG_FINE = 3
HG_NB = 2
HG_ROWS = (HG_NLEV + 2) * HG_C


def _hgrn_constants():
    C = HG_C
    W = np.zeros((2, HG_NLEV + 2, C, C), np.float32)
    M = np.zeros((2, HG_NLEV + 1, C, C), np.float32)
    for lv in range(HG_NLEV):
        m = C >> (lv + 1)
        for t in range(C):
            blk = t // m
            if blk % 2 == 1:
                p = blk * m
                W[0, lv, t, p:t + 1] = 1.0
                W[1, lv, t, p:t] = 1.0
            else:
                p = (blk + 1) * m
                W[0, lv, t, t + 1:p] = 1.0
                W[1, lv, t, t:p] = 1.0
        for t in range(C):
            for s in range(C):
                if (t // m) % 2 == 1 and s // m == t // m - 1:
                    M[0, lv, t, s] = 1.0
                    M[1, lv, s, t] = 1.0
    for t in range(C):
        W[0, HG_NLEV, t, :t + 1] = 1.0
        W[1, HG_NLEV, t, t:] = 1.0
        W[0, HG_NLEV + 1, t, t + 1:] = 1.0
        W[1, HG_NLEV + 1, t, :t] = 1.0
        M[:, HG_NLEV, t, t] = 1.0
    W = np.concatenate([W[:, HG_NLEV:HG_NLEV + 1], W[:, HG_NLEV - HG_FINE:HG_NLEV]], axis=1)
    W = W.reshape(2, (1 + HG_FINE) * C, C)
    lev = np.full((2, C, C), -1, np.int32)
    for lv in range(HG_NLEV + 1):
        lev[M[:, lv] > 0.5] = lv
    return np.concatenate([W, W], axis=-1), lev


def _hgrn_kernel(q_ref, f_ref, v_ref, og_ref, lbl_ref, ng_ref, w_ref, m_ref, y_ref, of_ref, st_ref, *, nt, layer):
    p = pl.program_id(1)
    j = pl.program_id(2)
    tile = _seq_tile(p, j, nt, CTX // TM)
    C = HG_C
    nchunk = TM // C

    @pl.when(j == 0)
    def _():
        st_ref[...] = jnp.zeros_like(st_ref)

    logits = lbl_ref[0]
    e = jnp.exp(logits - jnp.max(logits, axis=0, keepdims=True))
    soft = e / jnp.sum(e, axis=0, keepdims=True)
    lb = jnp.sum(soft[0:layer + 1], axis=0, keepdims=True) - soft[0:1]

    wmat = w_ref[0]
    lev = m_ref[0]
    row0 = pl.multiple_of(tile * TM, TM)
    inter = HG_NLEV * C

    def prepare(backward, bb):
        f = lb + (1.0 - lb) * jax.nn.sigmoid(f_ref[bb].astype(F32))
        g = jnp.log(f) * LOG2E
        k = (1.0 - f).astype(BF16)
        qr = q_ref[bb].astype(F32)
        q = (qr * jax.nn.sigmoid(qr)).astype(BF16)
        g1 = g.astype(BF16)
        g2 = (g - g1.astype(F32)).astype(BF16)
        g_split = jnp.concatenate(
            [jnp.concatenate([g1[c * C:(c + 1) * C], g2[c * C:(c + 1) * C]], axis=0) for c in range(nchunk)], axis=1)
        sums = jnp.dot(wmat, g_split, preferred_element_type=F32)
        run = sums[:C]
        coarse = []
        for lv in range(HG_NLEV - HG_FINE):
            m = C >> (lv + 1)
            blocks = []
            for kb in range(C // (2 * m)):
                edge = (2 * kb + 1) * m - (0 if backward else 1)
                ref = run[edge:edge + 1]
                early, late = run[2 * kb * m:(2 * kb + 1) * m], run[(2 * kb + 1) * m:(2 * kb + 2) * m]
                blocks += [early - ref, ref - late] if backward else [ref - early, late - ref]
            coarse.append(jnp.concatenate(blocks, axis=0))
        to_edge = (run[0:1] if backward else run[C - 1:C]) - run
        ex32 = jnp.exp2(jnp.concatenate(coarse + [sums[C:], run, to_edge], axis=0))
        dec_row = inter if backward else inter + C - 1
        return q, k, v_ref[bb], ex32.astype(BF16), ex32[dec_row:dec_row + 1]

    def direction(backward):
        nbat = q_ref.shape[0]
        prep = [prepare(backward, bb) for bb in range(nbat)]
        intra, q_in, upd = {}, {}, {}
        for c in range(nchunk):
            rows = slice(c * C, (c + 1) * C)
            for h in range(HG_H):
                hs = slice(h * HG_DK, (h + 1) * HG_DK)
                for bb in range(nbat):
                    q, k, v, ex, _ = prep[bb]
                    qh, kh, vh = q[rows, hs], k[rows, hs], v[rows, hs]
                    eh = ex[:, c * BW + h * HG_DK:c * BW + (h + 1) * HG_DK].reshape(HG_NLEV + 2, C, HG_DK)
                    qs = jnp.concatenate([qh[None] * eh[:HG_NLEV], qh[None]], axis=0)
                    ks = jnp.concatenate([kh[None] * eh[:HG_NLEV], kh[None]], axis=0)
                    att_l = jnp.einsum('lqd,lkd->lqk', qs, ks, preferred_element_type=F32)
                    att = jnp.zeros((C, C), F32)
                    for lv in range(HG_NLEV + 1):
                        att = jnp.where(lev == lv, att_l[lv], att)
                    intra[bb, c, h] = jnp.dot(att.astype(BF16), vh, preferred_element_type=F32)
                    q_in[bb, c, h] = qh * eh[HG_NLEV]
                    upd[bb, c, h] = lax.dot_general(vh, kh * eh[HG_NLEV + 1], (((0,), (0,)), ((), ())),
                                                    preferred_element_type=F32)
        st = [[st_ref[bb, h] for h in range(HG_H)] for bb in range(nbat)]
        outs = [[None] * nchunk for _ in range(nbat)]
        for c in (reversed(range(nchunk)) if backward else range(nchunk)):
            row = [[] for _ in range(nbat)]
            for h in range(HG_H):
                for bb in range(nbat):
                    dec = prep[bb][4][:, c * BW + h * HG_DK:c * BW + (h + 1) * HG_DK]
                    row[bb].append(intra[bb, c, h]
                                   + lax.dot_general(q_in[bb, c, h], st[bb][h].astype(BF16), (((1,), (1,)), ((), ())),
                                                     preferred_element_type=F32))
                    st[bb][h] = dec * st[bb][h] + upd[bb, c, h]
            for bb in range(nbat):
                outs[bb][c] = jnp.concatenate(row[bb], axis=-1)
        for bb in range(nbat):
            for h in range(HG_H):
                st_ref[bb, h] = st[bb][h]
        return [jnp.concatenate(o, axis=0) for o in outs]

    @pl.when(p == 0)
    def _():
        for bb, o in enumerate(direction(False)):
            of_ref[bb, pl.ds(row0, TM), :] = o

    @pl.when(p == 1)
    def _():
        for bb, o_b in enumerate(direction(True)):
            o = o_b + of_ref[bb, pl.ds(row0, TM), :]
            ys = []
            for h in range(HG_H):
                oh = o[:, h * HG_DK:(h + 1) * HG_DK]
                ys.append(oh * lax.rsqrt(jnp.mean(oh * oh, axis=-1, keepdims=True) + EPS))
            yn = jnp.concatenate(ys, axis=-1) * ng_ref[...]
            y_ref[bb] = (yn * jax.nn.sigmoid(og_ref[bb].astype(F32))).astype(y_ref.dtype)


def _hgrn2(pm, lb_logits, norm_g, layer):
    B, S, _ = pm.shape
    nt = S // TM
    tile = functools.partial(_seq_tile, nt=nt, nct=CTX // TM)
    w_np, m_np = _hgrn_constants()
    w_c = jnp.asarray(w_np, BF16)
    m_c = jnp.asarray(m_np, jnp.int32)
    nb = min(HG_NB, B)
    col = lambda cb: pl.BlockSpec((nb, TM, BW), lambda b, p, j: (b, tile(p, j), cb))
    return pl.pallas_call(
        functools.partial(_hgrn_kernel, nt=nt, layer=layer),
        grid=(B // nb, 2, nt),
        in_specs=[
            col(0),
            pl.BlockSpec((nb, TM, BW), lambda b, p, j: (b, tile(p, j), 1 + p)),
            col(3), col(4),
            pl.BlockSpec((1, DEPTH, BW), lambda b, p, j: (p, 0, 0)),
            pl.BlockSpec((1, BW), lambda b, p, j: (0, 0)),
            pl.BlockSpec((1, (1 + HG_FINE) * HG_C, 2 * HG_C), lambda b, p, j: (p, 0, 0)),
            pl.BlockSpec((1, HG_C, HG_C), lambda b, p, j: (p, 0, 0)),
        ],
        out_specs=pl.BlockSpec((nb, TM, BW), lambda b, p, j: (b, jnp.where(p == 0, CTX // TM - 1, tile(p, j)), 0)),
        out_shape=jax.ShapeDtypeStruct((B, S, BW), BF16),
        scratch_shapes=[
            pltpu.VMEM((nb, S, BW), F32),
            pltpu.VMEM((nb, HG_H, HG_DK, HG_DK), F32),
        ],
        compiler_params=_cp(("arbitrary", "arbitrary", "arbitrary")),
        name="hgrn2",
    )(pm, pm, pm, pm, lb_logits, norm_g, w_c, m_c)


NA_GH = 4
NA_GW = NA_GH * NA_DH


def _head_mask(h, n_rows):
    lane = lax.broadcasted_iota(jnp.int32, (n_rows, NA_GW), 1)
    return lane // NA_DH == h


def _softmax_pv(s_parts, v_parts):
    out = jnp.zeros((TM, NA_GW), F32)
    for h in range(NA_GH):
        parts = [s[h * TM:(h + 1) * TM] for s in s_parts]
        m = parts[0].max(axis=-1, keepdims=True)
        for s in parts[1:]:
            m = jnp.maximum(m, s.max(axis=-1, keepdims=True))
        ps = [jnp.exp2(s - m) for s in parts]
        den = ps[0].sum(axis=-1, keepdims=True)
        for pp in ps[1:]:
            den = den + pp.sum(axis=-1, keepdims=True)
        acc = jnp.zeros((TM, NA_GW), F32)
        for pp, vv in zip(ps, v_parts):
            acc = acc + jnp.dot(pp.astype(BF16), vv, preferred_element_type=F32)
        out = out + jnp.where(_head_mask(h, TM), acc * (1.0 / den), 0.0)
    return out.astype(BF16)


def _na_kernel(qr_ref, qp_ref, k0_ref, k1_ref, k2_ref, v0_ref, v1_ref, v2_ref, kc_ref, vc_ref, bias_ref, y_ref,
               *, with_ctx):
    step = pl.program_id(0)
    masks_q = [_head_mask(h, TM) for h in range(NA_GH)]
    rows = 2048 // GRID_W

    def stack_heads(q):
        return jnp.concatenate([jnp.where(masks_q[h], q, jnp.zeros_like(q)) for h in range(NA_GH)], axis=0)

    nt_dims = (((1,), (1,)), ((), ()))

    def bias_tile_index(iq, key_row):
        lat = step - 1 if with_ctx else step
        r = lat * NA_RB + iq
        u0 = _na_key_base(lat) * NA_RB
        d0 = jnp.clip(r - WIN_R // 2, 0, rows - WIN_R) - r + (WIN_R - 1)
        d = u0 + key_row - r + (WIN_R - 1)
        both = jnp.logical_and(d >= d0, d + 1 <= d0 + WIN_R - 1)
        variant = jnp.where(both, 0, jnp.where(d == d0 + WIN_R - 1, 1, jnp.where(d == d0 - 1, 2, 3)))
        return variant, jnp.clip(d + 1, 0, 2 * WIN_R - 1)

    def part_bias(g, i):
        tiles = {}
        for iq in range(NA_RB):
            for jp in range(TM // (2 * GRID_W)):
                tiles[iq, jp] = bias_tile_index(iq, i * NA_RB + 2 * jp)
        blocks = []
        for h in range(NA_GH):
            for iq in range(NA_RB):
                blocks.append(jnp.concatenate(
                    [bias_ref[tiles[iq, jp][0], g * NA_GH + h, tiles[iq, jp][1]]
                     for jp in range(TM // (2 * GRID_W))], axis=1))
        return jnp.concatenate(blocks, axis=0)

    def latent():
        for g in range(NA_H // NA_GH):
            gs = slice(g * NA_GW, (g + 1) * NA_GW)
            q_rot = stack_heads(qr_ref[0, :, gs])
            s_parts, v_parts = [], []
            for i, (k_ref, v_ref) in enumerate(((k0_ref, v0_ref), (k1_ref, v1_ref), (k2_ref, v2_ref))):
                s = lax.dot_general(q_rot, k_ref[0, :, gs], nt_dims, preferred_element_type=F32)
                s_parts.append(s + part_bias(g, i))
                v_parts.append(v_ref[0, :, gs])
            s_parts.append(lax.dot_general(stack_heads(qp_ref[0, :, gs]), kc_ref[0, :, gs], nt_dims,
                                           preferred_element_type=F32))
            v_parts.append(vc_ref[0, :, gs])
            y_ref[0, :, gs] = _softmax_pv(s_parts, v_parts)

    def context():
        for g in range(NA_H // NA_GH):
            gs = slice(g * NA_GW, (g + 1) * NA_GW)
            s_ctx = lax.dot_general(stack_heads(qp_ref[0, :, gs]), kc_ref[0, :, gs], nt_dims,
                                    preferred_element_type=F32)
            y_ref[0, :, gs] = _softmax_pv([s_ctx], [vc_ref[0, :, gs]])

    if with_ctx:
        pl.when(step == 0)(context)
        pl.when(step > 0)(latent)
    else:
        latent()


def _na_bias_tiles(rpb):
    col = np.arange(GRID_W)
    c_start = np.clip(col - WIN_C // 2, 0, GRID_W - WIN_C)
    col_ok = (col[None, :] >= c_start[:, None]) & (col[None, :] < c_start[:, None] + WIN_C)
    d_col = np.clip(col[None, :] - col[:, None], -(WIN_C - 1), WIN_C - 1) + (WIN_C - 1)
    hot = jnp.asarray(np.eye(2 * WIN_C - 1, dtype=np.float32)[d_col])
    by_col = jnp.einsum('hrc,qkc->hrqk', rpb.astype(F32), hot, precision=lax.Precision.HIGHEST) * LOG2E
    by_col = jnp.where(jnp.asarray(col_ok)[None, None], by_col, MASK_VALUE)
    masked = jnp.full((NA_H, 1, GRID_W, GRID_W), MASK_VALUE, F32)
    padded = jnp.concatenate([masked, by_col, masked], axis=1)
    first, second = padded[:, :-1], padded[:, 1:]
    none = jnp.broadcast_to(masked, first.shape)
    pair = lambda a, b: jnp.concatenate([a, b], axis=-1)
    return jnp.stack([pair(first, second), pair(first, none), pair(none, second), pair(none, none)])


def _na_key_base(step_latent):
    return jnp.clip(step_latent - 1, 0, 2048 // TM - NA_UR * GRID_W // TM)


def _attention(qr, qp, kr, pm, bias, with_ctx):
    B, S, _ = qr.shape
    nlat = (S - CTX) // TM
    off = 0 if with_ctx else 1

    def qtile(t, b):
        return (b, t + off, 0)

    def ktile(i, cb=0):
        def f(t, b):
            return (b, 1 + _na_key_base(t + off - 1) + i, cb)
        return f

    blk = lambda f: pl.BlockSpec((1, TM, BW), f)
    return pl.pallas_call(
        functools.partial(_na_kernel, with_ctx=with_ctx),
        grid=(nlat + (1 if with_ctx else 0), B),
        in_specs=[blk(qtile), blk(qtile), blk(ktile(0)), blk(ktile(1)), blk(ktile(2)),
                  blk(ktile(0, PM_V)), blk(ktile(1, PM_V)), blk(ktile(2, PM_V)),
                  blk(lambda t, b: (b, 0, 0)), blk(lambda t, b: (b, 0, PM_V)),
                  pl.BlockSpec(bias.shape, lambda t, b: (0,) * bias.ndim)],
        out_specs=blk(lambda t, b: (b, t, 0)),
        out_shape=jax.ShapeDtypeStruct((B, S - off * TM, BW), BF16),
        compiler_params=_cp(("arbitrary", "arbitrary")),
        name="attention",
    )(qr, qp, kr, kr, kr, pm, pm, pm, kr, pm, bias)


def _merge_ffn_kernel(c_ref, h_ref, mod_ref, ya_ref, yb_ref, yc_ref, g0_ref, g1_ref, g2_ref, wb_ref, wo_ref, ng_ref,
                      w1_ref, w2_ref, o_ref, *, first_tile):
    h_in = h_ref[0]
    if first_tile == 0:
        h_in = jnp.where(pl.program_id(0) == 0, c_ref[0], h_in)
    m = mod_ref[0]
    acc = None
    for n, (y_ref, g_ref) in enumerate(((ya_ref, g0_ref), (yb_ref, g1_ref), (yc_ref, g2_ref))):
        y = y_ref[...].reshape(TM, BW).astype(BF16)
        term = jax.nn.sigmoid(g_ref[0].astype(F32)) * jnp.dot(y, wb_ref[n], preferred_element_type=F32)
        acc = term if acc is None else acc + term
    mixed = jnp.dot(acc.astype(BF16), wo_ref[...], preferred_element_type=F32)
    x = h_in + m[:, 2 * D:3 * D] * mixed
    y = x * lax.rsqrt(jnp.mean(x * x, axis=-1, keepdims=True) + EPS) * ng_ref[...]
    u = (y * (1.0 + m[:, 4 * D:5 * D]) + m[:, 3 * D:4 * D]).astype(BF16)
    a = jnp.maximum(jnp.dot(u, w1_ref[...], preferred_element_type=F32), 0.0)
    out = jnp.dot((a * a).astype(BF16), w2_ref[...], preferred_element_type=F32)
    o_ref[0] = x + m[:, 5 * D:6 * D] * out


def _merge_ffn(ctx_arr, lat_arr, lat_first, mod_l, ya, yb, yc, pm, w_branch, w_out, norm_g, w1, w2, l, first_tile):
    B, S, _ = yb.shape
    nt = S // TM - first_tile
    tok = lambda w: pl.BlockSpec((1, TM, w), lambda j, b: (b, j + first_tile, 0))
    gate = lambda n: pl.BlockSpec((1, TM, D), lambda j, b: (b, j + first_tile, PM_GATE0 * BW // D + n))
    layer = lambda shape: pl.BlockSpec((None,) + shape, lambda j, b: (l,) + (0,) * len(shape),
                                       pipeline_mode=pl.Buffered(1))
    return pl.pallas_call(
        functools.partial(_merge_ffn_kernel, first_tile=first_tile),
        grid=(nt, B),
        in_specs=[
            *_stream_specs(lat_first, first_tile),
            pl.BlockSpec((1, 1, 6 * D), lambda j, b: (_mod_row(b, j + first_tile), 0, 0)),
            pl.BlockSpec((TM, BW), lambda j, b: (j + first_tile, b)),
            tok(BW),
            pl.BlockSpec((1, TM, BW), lambda j, b: (b, j, 0)),
            gate(0), gate(1), gate(2),
            layer((3, BW, D)), layer((D, D)), pl.BlockSpec((1, D), lambda j, b: (0, 0)),
            layer((D, D_FF)), layer((D_FF, D)),
        ],
        out_specs=pl.BlockSpec((1, TM, D), lambda j, b: (b, j, 0)),
        out_shape=jax.ShapeDtypeStruct((B, nt * TM, D), F32),
        compiler_params=_cp(("arbitrary", "arbitrary")),
        name="merge_ffn",
    )(ctx_arr, lat_arr, mod_l, ya, yb, yc, pm, pm, pm, w_branch, w_out, norm_g, w1, w2)


def kernel(x, c, ctx, c_ctx, ada_w, ada_b, norm1_g, norm2_g, w_in, conv_w, conv_b, lru_wa, lru_ba, lru_wx, lru_bx,
           lru_lambda, hg_lb_logits, hg_norm_g, na_q_norm_g, na_k_norm_g, na_rpb, w_branch, w_out, ffn_w1, ffn_w2):
    B, T, _ = x.shape
    S = CTX + T
    cond = jnp.zeros((16, D), F32).at[:B].set(c).at[8].set(c_ctx)
    mod = _modulation(cond, ada_w, ada_b)
    w_in_b, w_branch_b, w_out_b = w_in.astype(BF16), w_branch.astype(BF16), w_out.astype(BF16)
    ffn_w1_b, ffn_w2_b = ffn_w1.astype(BF16), ffn_w2.astype(BF16)
    stream = (ctx, x, 1)
    for l in range(DEPTH):
        last = l == DEPTH - 1
        mod_l = mod[l].reshape(16, 1, 6 * D)
        pm, pa, qr, qp, kr = _projection(*stream, mod_l, norm1_g[l][None], w_in_b, l,
                                         jnp.tile(na_q_norm_g[l], NA_H)[None], jnp.tile(na_k_norm_g[l], NA_H)[None])

        ya = _rglru(pa, conv_w[l], conv_b[l][None],
                    _block_diag(lru_wa[l]).astype(BF16), _block_diag(lru_wx[l]).astype(BF16),
                    lru_ba[l][:, None], lru_bx[l][:, None], lru_lambda[l][:, None])
        yb = _hgrn2(pm, hg_lb_logits, jnp.tile(hg_norm_g[l], HG_H)[None], l)
        yc = _attention(qr, qp, kr, pm, _na_bias_tiles(na_rpb[l]), with_ctx=not last)

        h = _merge_ffn(*stream, mod_l, ya.reshape(S, B * BW), yb, yc, pm, w_branch_b, w_out_b, norm2_g[l][None],
                       ffn_w1_b, ffn_w2_b, l, 1 if last else 0)
        stream = (h, h, 0)
    return h
```

```python
import functools

import numpy as np
import jax
import jax.numpy as jnp
from jax import lax
from jax.experimental import pallas as pl
from jax.experimental.pallas import tpu as pltpu

F32 = jnp.float32
BF16 = jnp.bfloat16

D = 1024
DEPTH = 2
GRID_W = 64
CTX = 256
BW = 512
LRU_NB = 8
LRU_BS = BW // LRU_NB
LRU_C = 8.0
HG_H = 4
HG_DK = 128
HG_C = 64
NA_H = 8
NA_DH = 64
WIN_R = 8
WIN_C = 16
ROPE_BASE = 10000.0
MASK_VALUE = -1e30
D_FF = 4 * D
EPS = 1e-6
TM = 256
NA_RB = 4
NA_UR = 12
VMEM_LIMIT = 56 * 1024 * 1024
LOG2E = 1.4426950408889634


def _cp(sem):
    return pltpu.CompilerParams(dimension_semantics=sem, vmem_limit_bytes=VMEM_LIMIT)


def _mod_row(b, j):
    return jnp.where(j == 0, 8, b)


def _mod_kernel(c_ref, w_ref, b_ref, o_ref):
    c = c_ref[...]
    s = c * jax.nn.sigmoid(c)
    o_ref[0] = jnp.dot(s, w_ref[0], preferred_element_type=F32) + b_ref[0]


def _modulation(cond, ada_w, ada_b):
    tn = 1536
    return pl.pallas_call(
        _mod_kernel,
        grid=(DEPTH, 6 * D // tn),
        in_specs=[
            pl.BlockSpec((16, D), lambda l, n: (0, 0)),
            pl.BlockSpec((1, D, tn), lambda l, n: (l, 0, n)),
            pl.BlockSpec((1, 1, tn), lambda l, n: (l, 0, n)),
        ],
        out_specs=pl.BlockSpec((1, 16, tn), lambda l, n: (l, 0, n)),
        out_shape=jax.ShapeDtypeStruct((DEPTH, 16, 6 * D), F32),
        compiler_params=_cp(("arbitrary", "arbitrary")),
        name="modulation",
    )(cond, ada_w, ada_b.reshape(DEPTH, 1, 6 * D))


CAST_BLOCK_ELEMS = 2 * 1024 * 1024


def _cast_kernel(x_ref, o_ref):
    o_ref[...] = x_ref[...].astype(o_ref.dtype)


def _to_bf16(w):
    L, R, C = w.shape
    br = min(R, CAST_BLOCK_ELEMS // C)
    return pl.pallas_call(
        _cast_kernel,
        grid=(L, R // br),
        in_specs=[pl.BlockSpec((1, br, C), lambda l, r: (l, r, 0))],
        out_specs=pl.BlockSpec((1, br, C), lambda l, r: (l, r, 0)),
        out_shape=jax.ShapeDtypeStruct(w.shape, BF16),
        compiler_params=_cp(("arbitrary", "arbitrary")),
        name="to_bf16",
    )(w)


def _seg_mean_sq(x, seg):
    s = jnp.dot((x * x).astype(BF16), seg, preferred_element_type=F32)
    return s * (1.0 / NA_DH)


def _swap16(y):
    n = y.shape[-1]
    lane = lax.broadcasted_iota(jnp.int32, y.shape, 1)
    return jnp.where(lane % 32 < 16, pltpu.roll(y, n - 16, 1), pltpu.roll(y, 16, 1))


def _rope_tables(S):
    half = NA_DH // 2
    nf = half // 2
    t = np.arange(S - CTX)
    inv_freq = (ROPE_BASE ** (-np.arange(nf, dtype=np.float32) / nf)).astype(np.float32)
    ang_r = (t // GRID_W).astype(np.float32)[:, None] * inv_freq
    ang_c = (t % GRID_W).astype(np.float32)[:, None] * inv_freq
    ang = np.concatenate([ang_r, ang_r, ang_c, ang_c], axis=-1)
    sign = np.concatenate([-np.ones(nf), np.ones(nf), -np.ones(nf), np.ones(nf)]).astype(np.float32)
    cos = np.concatenate([np.ones((CTX, NA_DH), np.float32), np.cos(ang)], axis=0)
    sin = np.concatenate([np.zeros((CTX, NA_DH), np.float32), np.sin(ang) * sign], axis=0)
    return np.tile(cos, (1, NA_H)).astype(np.float32), np.tile(sin, (1, NA_H)).astype(np.float32)


NA_Q_SCALE = NA_DH ** -0.5 * LOG2E
PM_V = 5
PM_GATE0 = 6
PM_COLS = 12 * BW


def _proj_kernel(c_ref, x_ref, mod_ref, g_ref, w_ref, qg_ref, kg_ref, seg_ref, cos_ref, sin_ref,
                 p_ref, pa_ref, qr_ref, qp_ref, kr_ref):
    j = pl.program_id(0)
    b = pl.program_id(1)
    x = jnp.where(j == 0, c_ref[0], x_ref[0])
    m = mod_ref[0]
    y = x * lax.rsqrt(jnp.mean(x * x, axis=-1, keepdims=True) + EPS) * g_ref[...]
    u = (y * (1.0 + m[:, D:2 * D]) + m[:, 0:D]).astype(BF16)
    dot = lambda lo, hi: jnp.dot(u, w_ref[:, lo * BW:hi * BW], preferred_element_type=F32)
    seg = seg_ref[...]
    cos = cos_ref[...]
    sin = sin_ref[...]
    q = dot(7, 8)
    qn = q * lax.rsqrt(_seg_mean_sq(q, seg) + EPS) * qg_ref[...]
    qp_ref[0] = (qn * NA_Q_SCALE).astype(BF16)
    qr_ref[0] = ((qn * cos + _swap16(qn) * sin) * NA_Q_SCALE).astype(BF16)
    k = dot(8, 9)
    kn = k * lax.rsqrt(_seg_mean_sq(k, seg) + EPS) * kg_ref[...]
    kr_ref[0] = (kn * cos + _swap16(kn) * sin).astype(BF16)
    pa_ref[:, pl.ds(b, 1), :] = dot(0, 2)[:, None, :]
    p_ref[0, :, :5 * BW] = dot(2, 7).astype(p_ref.dtype)
    p_ref[0, :, 5 * BW:] = dot(9, 16).astype(p_ref.dtype)


def _stream_specs(lat_first, first_tile=0):
    return (pl.BlockSpec((1, TM, D), lambda j, b: (jnp.where(j + first_tile == 0, b, 0), 0, 0)),
            pl.BlockSpec((1, TM, D), lambda j, b: (b, jnp.maximum(j + first_tile - lat_first, 0), 0)))


def _projection(ctx_arr, lat_arr, lat_first, mod_l, g, w, l, q_g, k_g):
    B = lat_arr.shape[0]
    S = lat_arr.shape[1] + lat_first * TM
    nt = S // TM
    cos, sin = _rope_tables(S)
    seg = np.kron(np.eye(NA_H, dtype=np.float32), np.ones((NA_DH, NA_DH), np.float32))
    const = lambda shape: pl.BlockSpec(shape, lambda j, b: (0,) * len(shape))
    tok = lambda w_: pl.BlockSpec((1, TM, w_), lambda j, b: (b, j, 0))
    return pl.pallas_call(
        _proj_kernel,
        grid=(nt, B),
        in_specs=[
            *_stream_specs(lat_first),
            pl.BlockSpec((1, 1, 6 * D), lambda j, b: (_mod_row(b, j), 0, 0)),
            const((1, D)),
            pl.BlockSpec((None,) + w.shape[1:], lambda j, b: (l, 0, 0), pipeline_mode=pl.Buffered(1)),
            const((1, BW)), const((1, BW)), const((BW, BW)),
            pl.BlockSpec((TM, BW), lambda j, b: (j, 0)), pl.BlockSpec((TM, BW), lambda j, b: (j, 0)),
        ],
        out_specs=[tok(PM_COLS), pl.BlockSpec((TM, B, 2 * BW), lambda j, b: (j, 0, 0)), tok(BW), tok(BW), tok(BW)],
        out_shape=[
            jax.ShapeDtypeStruct((B, S, PM_COLS), BF16),
            jax.ShapeDtypeStruct((S, B, 2 * BW), F32),
            jax.ShapeDtypeStruct((B, S, BW), BF16),
            jax.ShapeDtypeStruct((B, S, BW), BF16),
            jax.ShapeDtypeStruct((B, S, BW), BF16),
        ],
        compiler_params=_cp(("arbitrary", "arbitrary")),
        name="projection",
    )(ctx_arr, lat_arr, mod_l, g, w, q_g, k_g, jnp.asarray(seg, BF16), jnp.asarray(cos), jnp.asarray(sin))


LRU_CW = 256
LRU_TM = 128


def _seq_tile(p, j, nt, nct):
    return jnp.where(p == 0, j, jnp.where(j < nct, nct - 1 - j, nt - 1 - (j - nct)))


def _softplus(x):
    return jnp.maximum(x, 0.0) + jnp.log1p(jnp.exp(-jnp.abs(x)))


def _gelu_tanh(x):
    return 0.5 * x * (1.0 + jnp.tanh(np.sqrt(2.0 / np.pi) * (x + 0.044715 * (x * x * x))))


def _lru_kernel(x_ref, gate_ref, prev_ref, next_ref, cw_ref, cb_ref, wa_ref, wx_ref, ba_ref, bx_ref, lam_ref,
                y_ref, hf_ref, a_ref, b_ref, carry_ref, *, nt, nct):
    TM = LRU_TM
    p = pl.program_id(1)
    j = pl.program_id(2)
    tile = _seq_tile(p, j, nt, nct)
    nb = x_ref.shape[1]

    @pl.when(j == 0)
    def _():
        carry_ref[...] = jnp.zeros_like(carry_ref)

    x = x_ref[...]
    has_prev = jnp.logical_and(tile != 0, tile != nct)
    has_next = jnp.logical_and(tile != nct - 1, tile != nt - 1)
    prev = jnp.where(has_prev, prev_ref[...], 0.0)
    nxt = jnp.where(has_next, next_ref[...], 0.0)
    xe = jnp.concatenate([prev, x, nxt], axis=0)
    cw = cw_ref[...]
    xc = cb_ref[...][None] + xe[0:TM] * cw[0:1][None] + xe[1:TM + 1] * cw[1:2][None] \
        + xe[2:TM + 2] * cw[2:3][None] + xe[3:TM + 3] * cw[3:4][None]
    xc2 = xc.reshape(TM * nb, LRU_CW)
    xb = xc2.astype(BF16)
    r = jax.nn.sigmoid(jnp.dot(xb, wa_ref[0, 0], preferred_element_type=F32) + ba_ref[0])
    i = jax.nn.sigmoid(jnp.dot(xb, wx_ref[0, 0], preferred_element_type=F32) + bx_ref[0])
    log_a = (-LRU_C) * r * _softplus(-lam_ref[0])
    a = jnp.exp(log_a)
    bc = jnp.sqrt(1.0 - a * a) * (i * xc2)
    a_ref[...] = a.reshape(TM, nb, LRU_CW)
    b_ref[...] = bc.reshape(TM, nb, LRU_CW)

    def step(t, h):
        h = a_ref[t] * h + b_ref[t]
        b_ref[t] = h
        return h

    @pl.when(p == 0)
    def _():
        carry_ref[...] = lax.fori_loop(0, TM, step, carry_ref[...], unroll=8)
        hf_ref[pl.ds(pl.multiple_of(tile * TM, TM), TM)] = b_ref[...]

    @pl.when(p == 1)
    def _():
        carry_ref[...] = lax.fori_loop(0, TM, lambda s, h: step(TM - 1 - s, h), carry_ref[...], unroll=8)
        hf = hf_ref[pl.ds(pl.multiple_of(tile * TM, TM), TM)]
        y_ref[...] = (hf + b_ref[...]) * _gelu_tanh(gate_ref[...])


def _rglru(pa, conv_w, conv_b, wa_bd, wx_bd, ba, bx, lam):
    S, B, _ = pa.shape
    TM = LRU_TM
    nt = S // TM
    nct = CTX // TM
    ncw = BW // LRU_CW
    tile = functools.partial(_seq_tile, nt=nt, nct=nct)
    vec = lambda: pl.BlockSpec((1, 1, LRU_CW), lambda c, p, j: (p, 0, c))
    return pl.pallas_call(
        functools.partial(_lru_kernel, nt=nt, nct=nct),
        grid=(ncw, 2, nt),
        in_specs=[
            pl.BlockSpec((TM, B, LRU_CW), lambda c, p, j: (tile(p, j), 0, c)),
            pl.BlockSpec((TM, B, LRU_CW), lambda c, p, j: (tile(p, j), 0, ncw + c)),
            pl.BlockSpec((2, B, LRU_CW), lambda c, p, j: (jnp.maximum(tile(p, j) * (TM // 2) - 1, 0), 0, c)),
            pl.BlockSpec((1, B, LRU_CW), lambda c, p, j: (jnp.minimum((tile(p, j) + 1) * TM, S - 1), 0, c)),
            pl.BlockSpec((4, LRU_CW), lambda c, p, j: (0, c)),
            pl.BlockSpec((1, LRU_CW), lambda c, p, j: (0, c)),
            pl.BlockSpec((1, 1, LRU_CW, LRU_CW), lambda c, p, j: (p, c, 0, 0)),
            pl.BlockSpec((1, 1, LRU_CW, LRU_CW), lambda c, p, j: (p, c, 0, 0)),
            vec(), vec(), vec(),
        ],
        out_specs=pl.BlockSpec((TM, B, LRU_CW), lambda c, p, j: (jnp.where(p == 0, nct - 1, tile(p, j)), 0, c)),
        out_shape=jax.ShapeDtypeStruct((S, B, BW), F32),
        scratch_shapes=[
            pltpu.VMEM((S, B, LRU_CW), F32),
            pltpu.VMEM((TM, B, LRU_CW), F32),
            pltpu.VMEM((TM, B, LRU_CW), F32),
            pltpu.VMEM((B, LRU_CW), F32),
        ],
        compiler_params=_cp(("arbitrary", "arbitrary", "arbitrary")),
        name="rglru",
    )(pa, pa, pa, pa, conv_w, conv_b, wa_bd, wx_bd, ba, bx, lam)


def _block_diag(w):
    per = LRU_CW // LRU_BS
    w = w.reshape(2, BW // LRU_CW, per, LRU_BS, LRU_BS)
    eye = jnp.eye(per, dtype=w.dtype)
    out = jnp.einsum('dcnio,nm->dcnimo', w, eye)
    return out.reshape(2, BW // LRU_CW, LRU_CW, LRU_CW)


HG_NLEV = 6
HG_FINE = 3
HG_NB = 4
HG_ROWS = (HG_NLEV + 2) * HG_C


def _hgrn_constants():
    C = HG_C
    W = np.zeros((2, HG_NLEV + 2, C, C), np.float32)
    M = np.zeros((2, HG_NLEV + 1, C, C), np.float32)
    for lv in range(HG_NLEV):
        m = C >> (lv + 1)
        for t in range(C):
            blk = t // m
            if blk % 2 == 1:
                p = blk * m
                W[0, lv, t, p:t + 1] = 1.0
                W[1, lv, t, p:t] = 1.0
            else:
                p = (blk + 1) * m
                W[0, lv, t, t + 1:p] = 1.0
                W[1, lv, t, t:p] = 1.0
        for t in range(C):
            for s in range(C):
                if (t // m) % 2 == 1 and s // m == t // m - 1:
                    M[0, lv, t, s] = 1.0
                    M[1, lv, s, t] = 1.0
    for t in range(C):
        W[0, HG_NLEV, t, :t + 1] = 1.0
        W[1, HG_NLEV, t, t:] = 1.0
        W[0, HG_NLEV + 1, t, t + 1:] = 1.0
        W[1, HG_NLEV + 1, t, :t] = 1.0
        M[:, HG_NLEV, t, t] = 1.0
    W = np.concatenate([W[:, HG_NLEV:HG_NLEV + 1], W[:, HG_NLEV - HG_FINE:HG_NLEV]], axis=1)
    W = W.reshape(2, (1 + HG_FINE) * C, C)
    lev = np.full((2, C, C), -1, np.int32)
    for lv in range(HG_NLEV + 1):
        lev[M[:, lv] > 0.5] = lv
    return np.concatenate([W, W], axis=-1), lev


def _hgrn_kernel(q_ref, f_ref, v_ref, og_ref, lbl_ref, ng_ref, w_ref, m_ref, y_ref, of_ref, st_ref, *, nt, layer):
    p = pl.program_id(1)
    j = pl.program_id(2)
    tile = _seq_tile(p, j, nt, CTX // TM)
    C = HG_C
    nchunk = TM // C

    @pl.when(j == 0)
    def _():
        st_ref[...] = jnp.zeros_like(st_ref)

    logits = lbl_ref[0]
    e = jnp.exp(logits - jnp.max(logits, axis=0, keepdims=True))
    soft = e / jnp.sum(e, axis=0, keepdims=True)
    lb = jnp.sum(soft[0:layer + 1], axis=0, keepdims=True) - soft[0:1]

    wmat = w_ref[0]
    lev = m_ref[0]
    row0 = pl.multiple_of(tile * TM, TM)
    inter = HG_NLEV * C

    def prepare(backward, bb):
        f = lb + (1.0 - lb) * jax.nn.sigmoid(f_ref[bb].astype(F32))
        g = jnp.log(f) * LOG2E
        k = (1.0 - f).astype(BF16)
        qr = q_ref[bb].astype(F32)
        q = (qr * jax.nn.sigmoid(qr)).astype(BF16)
        g1 = g.astype(BF16)
        g2 = (g - g1.astype(F32)).astype(BF16)
        g_split = jnp.concatenate(
            [jnp.concatenate([g1[c * C:(c + 1) * C], g2[c * C:(c + 1) * C]], axis=0) for c in range(nchunk)], axis=1)
        sums = jnp.dot(wmat, g_split, preferred_element_type=F32)
        run = sums[:C]
        coarse = []
        for lv in range(HG_NLEV - HG_FINE):
            m = C >> (lv + 1)
            blocks = []
            for kb in range(C // (2 * m)):
                edge = (2 * kb + 1) * m - (0 if backward else 1)
                ref = run[edge:edge + 1]
                early, late = run[2 * kb * m:(2 * kb + 1) * m], run[(2 * kb + 1) * m:(2 * kb + 2) * m]
                blocks += [early - ref, ref - late] if backward else [ref - early, late - ref]
            coarse.append(jnp.concatenate(blocks, axis=0))
        to_edge = (run[0:1] if backward else run[C - 1:C]) - run
        ex32 = jnp.exp2(jnp.concatenate(coarse + [sums[C:], run, to_edge], axis=0))
        dec_row = inter if backward else inter + C - 1
        return q, k, v_ref[bb], ex32.astype(BF16), ex32[dec_row:dec_row + 1]

    def direction(backward):
        nbat = q_ref.shape[0]
        prep = [prepare(backward, bb) for bb in range(nbat)]
        intra, q_in, upd = {}, {}, {}
        for c in range(nchunk):
            rows = slice(c * C, (c + 1) * C)
            for h in range(HG_H):
                hs = slice(h * HG_DK, (h + 1) * HG_DK)
                for bb in range(nbat):
                    q, k, v, ex, _ = prep[bb]
                    qh, kh, vh = q[rows, hs], k[rows, hs], v[rows, hs]
                    eh = ex[:, c * BW + h * HG_DK:c * BW + (h + 1) * HG_DK].reshape(HG_NLEV + 2, C, HG_DK)
                    qs = jnp.concatenate([qh[None] * eh[:HG_NLEV], qh[None]], axis=0)
                    ks = jnp.concatenate([kh[None] * eh[:HG_NLEV], kh[None]], axis=0)
                    att_l = jnp.einsum('lqd,lkd->lqk', qs, ks, preferred_element_type=F32)
                    att = jnp.zeros((C, C), F32)
                    for lv in range(HG_NLEV + 1):
                        att = jnp.where(lev == lv, att_l[lv], att)
                    intra[bb, c, h] = jnp.dot(att.astype(BF16), vh, preferred_element_type=F32)
                    q_in[bb, c, h] = qh * eh[HG_NLEV]
                    upd[bb, c, h] = lax.dot_general(vh, kh * eh[HG_NLEV + 1], (((0,), (0,)), ((), ())),
                                                    preferred_element_type=F32)
        st = [[st_ref[bb, h] for h in range(HG_H)] for bb in range(nbat)]
        outs = [[None] * nchunk for _ in range(nbat)]
        for c in (reversed(range(nchunk)) if backward else range(nchunk)):
            row = [[] for _ in range(nbat)]
            for h in range(HG_H):
                for bb in range(nbat):
                    dec = prep[bb][4][:, c * BW + h * HG_DK:c * BW + (h + 1) * HG_DK]
                    row[bb].append(intra[bb, c, h]
                                   + lax.dot_general(q_in[bb, c, h], st[bb][h].astype(BF16), (((1,), (1,)), ((), ())),
                                                     preferred_element_type=F32))
                    st[bb][h] = dec * st[bb][h] + upd[bb, c, h]
            for bb in range(nbat):
                outs[bb][c] = jnp.concatenate(row[bb], axis=-1)
        for bb in range(nbat):
            for h in range(HG_H):
                st_ref[bb, h] = st[bb][h]
        return [jnp.concatenate(o, axis=0) for o in outs]

    @pl.when(p == 0)
    def _():
        for bb, o in enumerate(direction(False)):
            of_ref[bb, pl.ds(row0, TM), :] = o

    @pl.when(p == 1)
    def _():
        for bb, o_b in enumerate(direction(True)):
            o = o_b + of_ref[bb, pl.ds(row0, TM), :]
            ys = []
            for h in range(HG_H):
                oh = o[:, h * HG_DK:(h + 1) * HG_DK]
                ys.append(oh * lax.rsqrt(jnp.mean(oh * oh, axis=-1, keepdims=True) + EPS))
            yn = jnp.concatenate(ys, axis=-1) * ng_ref[...]
            y_ref[bb] = (yn * jax.nn.sigmoid(og_ref[bb].astype(F32))).astype(y_ref.dtype)


def _hgrn2(pm, lb_logits, norm_g, layer):
    B, S, _ = pm.shape
    nt = S // TM
    tile = functools.partial(_seq_tile, nt=nt, nct=CTX // TM)
    w_np, m_np = _hgrn_constants()
    w_c = jnp.asarray(w_np, BF16)
    m_c = jnp.asarray(m_np, jnp.int32)
    nb = min(HG_NB, B)
    col = lambda cb: pl.BlockSpec((nb, TM, BW), lambda b, p, j: (b, tile(p, j), cb))
    return pl.pallas_call(
        functools.partial(_hgrn_kernel, nt=nt, layer=layer),
        grid=(B // nb, 2, nt),
        in_specs=[
            col(0),
            pl.BlockSpec((nb, TM, BW), lambda b, p, j: (b, tile(p, j), 1 + p)),
            col(3), col(4),
            pl.BlockSpec((1, DEPTH, BW), lambda b, p, j: (p, 0, 0)),
            pl.BlockSpec((1, BW), lambda b, p, j: (0, 0)),
            pl.BlockSpec((1, (1 + HG_FINE) * HG_C, 2 * HG_C), lambda b, p, j: (p, 0, 0)),
            pl.BlockSpec((1, HG_C, HG_C), lambda b, p, j: (p, 0, 0)),
        ],
        out_specs=pl.BlockSpec((nb, TM, BW), lambda b, p, j: (b, jnp.where(p == 0, CTX // TM - 1, tile(p, j)), 0)),
        out_shape=jax.ShapeDtypeStruct((B, S, BW), BF16),
        scratch_shapes=[
            pltpu.VMEM((nb, S, BW), F32),
            pltpu.VMEM((nb, HG_H, HG_DK, HG_DK), F32),
        ],
        compiler_params=_cp(("arbitrary", "arbitrary", "arbitrary")),
        name="hgrn2",
    )(pm, pm, pm, pm, lb_logits, norm_g, w_c, m_c)


NA_GH = 4
NA_GW = NA_GH * NA_DH


def _head_mask(h, n_rows):
    lane = lax.broadcasted_iota(jnp.int32, (n_rows, NA_GW), 1)
    return lane // NA_DH == h


def _softmax_pv(s_parts, v_parts):
    out = jnp.zeros((TM, NA_GW), F32)
    for h in range(NA_GH):
        parts = [s[h * TM:(h + 1) * TM] for s in s_parts]
        m = parts[0].max(axis=-1, keepdims=True)
        for s in parts[1:]:
            m = jnp.maximum(m, s.max(axis=-1, keepdims=True))
        ps = [jnp.exp2(s - m) for s in parts]
        den = ps[0].sum(axis=-1, keepdims=True)
        for pp in ps[1:]:
            den = den + pp.sum(axis=-1, keepdims=True)
        acc = jnp.zeros((TM, NA_GW), F32)
        for pp, vv in zip(ps, v_parts):
            acc = acc + jnp.dot(pp.astype(BF16), vv, preferred_element_type=F32)
        out = out + jnp.where(_head_mask(h, TM), acc * (1.0 / den), 0.0)
    return out.astype(BF16)


def _na_kernel(qr_ref, qp_ref, k0_ref, k1_ref, k2_ref, v0_ref, v1_ref, v2_ref, kc_ref, vc_ref, bias_ref, y_ref,
               *, with_ctx):
    step = pl.program_id(0)
    masks_q = [_head_mask(h, TM) for h in range(NA_GH)]
    rows = 2048 // GRID_W

    def stack_heads(q):
        return jnp.concatenate([jnp.where(masks_q[h], q, jnp.zeros_like(q)) for h in range(NA_GH)], axis=0)

    nt_dims = (((1,), (1,)), ((), ()))

    def bias_tile_index(iq, key_row):
        lat = step - 1 if with_ctx else step
        r = lat * NA_RB + iq
        u0 = _na_key_base(lat) * NA_RB
        d0 = jnp.clip(r - WIN_R // 2, 0, rows - WIN_R) - r + (WIN_R - 1)
        d = u0 + key_row - r + (WIN_R - 1)
        both = jnp.logical_and(d >= d0, d + 1 <= d0 + WIN_R - 1)
        variant = jnp.where(both, 0, jnp.where(d == d0 + WIN_R - 1, 1, jnp.where(d == d0 - 1, 2, 3)))
        return variant, jnp.clip(d + 1, 0, 2 * WIN_R - 1)

    def part_bias(g, i):
        tiles = {}
        for iq in range(NA_RB):
            for jp in range(TM // (2 * GRID_W)):
                tiles[iq, jp] = bias_tile_index(iq, i * NA_RB + 2 * jp)
        blocks = []
        for h in range(NA_GH):
            for iq in range(NA_RB):
                blocks.append(jnp.concatenate(
                    [bias_ref[tiles[iq, jp][0], g * NA_GH + h, tiles[iq, jp][1]]
                     for jp in range(TM // (2 * GRID_W))], axis=1))
        return jnp.concatenate(blocks, axis=0)

    def latent():
        for g in range(NA_H // NA_GH):
            gs = slice(g * NA_GW, (g + 1) * NA_GW)
            q_rot = stack_heads(qr_ref[0, :, gs])
            s_parts, v_parts = [], []
            for i, (k_ref, v_ref) in enumerate(((k0_ref, v0_ref), (k1_ref, v1_ref), (k2_ref, v2_ref))):
                s = lax.dot_general(q_rot, k_ref[0, :, gs], nt_dims, preferred_element_type=F32)
                s_parts.append(s + part_bias(g, i))
                v_parts.append(v_ref[0, :, gs])
            s_parts.append(lax.dot_general(stack_heads(qp_ref[0, :, gs]), kc_ref[0, :, gs], nt_dims,
                                           preferred_element_type=F32))
            v_parts.append(vc_ref[0, :, gs])
            y_ref[0, :, gs] = _softmax_pv(s_parts, v_parts)

    def context():
        for g in range(NA_H // NA_GH):
            gs = slice(g * NA_GW, (g + 1) * NA_GW)
            s_ctx = lax.dot_general(stack_heads(qp_ref[0, :, gs]), kc_ref[0, :, gs], nt_dims,
                                    preferred_element_type=F32)
            y_ref[0, :, gs] = _softmax_pv([s_ctx], [vc_ref[0, :, gs]])

    if with_ctx:
        pl.when(step == 0)(context)
        pl.when(step > 0)(latent)
    else:
        latent()


def _na_bias_tiles(rpb):
    col = np.arange(GRID_W)
    c_start = np.clip(col - WIN_C // 2, 0, GRID_W - WIN_C)
    col_ok = (col[None, :] >= c_start[:, None]) & (col[None, :] < c_start[:, None] + WIN_C)
    d_col = np.clip(col[None, :] - col[:, None], -(WIN_C - 1), WIN_C - 1) + (WIN_C - 1)
    hot = jnp.asarray(np.eye(2 * WIN_C - 1, dtype=np.float32)[d_col])
    by_col = jnp.einsum('hrc,qkc->hrqk', rpb.astype(F32), hot, precision=lax.Precision.HIGHEST) * LOG2E
    by_col = jnp.where(jnp.asarray(col_ok)[None, None], by_col, MASK_VALUE)
    masked = jnp.full((NA_H, 1, GRID_W, GRID_W), MASK_VALUE, F32)
    padded = jnp.concatenate([masked, by_col, masked], axis=1)
    first, second = padded[:, :-1], padded[:, 1:]
    none = jnp.broadcast_to(masked, first.shape)
    pair = lambda a, b: jnp.concatenate([a, b], axis=-1)
    return jnp.stack([pair(first, second), pair(first, none), pair(none, second), pair(none, none)])


def _na_key_base(step_latent):
    return jnp.clip(step_latent - 1, 0, 2048 // TM - NA_UR * GRID_W // TM)


def _attention(qr, qp, kr, pm, bias, with_ctx):
    B, S, _ = qr.shape
    nlat = (S - CTX) // TM
    off = 0 if with_ctx else 1

    def qtile(t, b):
        return (b, t + off, 0)

    def ktile(i, cb=0):
        def f(t, b):
            return (b, 1 + _na_key_base(t + off - 1) + i, cb)
        return f

    blk = lambda f: pl.BlockSpec((1, TM, BW), f)
    return pl.pallas_call(
        functools.partial(_na_kernel, with_ctx=with_ctx),
        grid=(nlat + (1 if with_ctx else 0), B),
        in_specs=[blk(qtile), blk(qtile), blk(ktile(0)), blk(ktile(1)), blk(ktile(2)),
                  blk(ktile(0, PM_V)), blk(ktile(1, PM_V)), blk(ktile(2, PM_V)),
                  blk(lambda t, b: (b, 0, 0)), blk(lambda t, b: (b, 0, PM_V)),
                  pl.BlockSpec(bias.shape, lambda t, b: (0,) * bias.ndim)],
        out_specs=blk(lambda t, b: (b, t, 0)),
        out_shape=jax.ShapeDtypeStruct((B, S - off * TM, BW), BF16),
        compiler_params=_cp(("arbitrary", "arbitrary")),
        name="attention",
    )(qr, qp, kr, kr, kr, pm, pm, pm, kr, pm, bias)


def _merge_ffn_kernel(c_ref, h_ref, mod_ref, ya_ref, yb_ref, yc_ref, g0_ref, g1_ref, g2_ref, wb_ref, wo_ref, ng_ref,
                      w1_ref, w2_ref, o_ref, *, first_tile):
    h_in = h_ref[0]
    if first_tile == 0:
        h_in = jnp.where(pl.program_id(0) == 0, c_ref[0], h_in)
    m = mod_ref[0]
    acc = None
    for n, (y_ref, g_ref) in enumerate(((ya_ref, g0_ref), (yb_ref, g1_ref), (yc_ref, g2_ref))):
        y = y_ref[...].reshape(TM, BW).astype(BF16)
        term = jax.nn.sigmoid(g_ref[0].astype(F32)) * jnp.dot(y, wb_ref[n], preferred_element_type=F32)
        acc = term if acc is None else acc + term
    mixed = jnp.dot(acc.astype(BF16), wo_ref[...], preferred_element_type=F32)
    x = h_in + m[:, 2 * D:3 * D] * mixed
    y = x * lax.rsqrt(jnp.mean(x * x, axis=-1, keepdims=True) + EPS) * ng_ref[...]
    u = (y * (1.0 + m[:, 4 * D:5 * D]) + m[:, 3 * D:4 * D]).astype(BF16)
    a = jnp.maximum(jnp.dot(u, w1_ref[...], preferred_element_type=F32), 0.0)
    out = jnp.dot((a * a).astype(BF16), w2_ref[...], preferred_element_type=F32)
    o_ref[0] = x + m[:, 5 * D:6 * D] * out


def _merge_ffn(ctx_arr, lat_arr, lat_first, mod_l, ya, yb, yc, pm, w_branch, w_out, norm_g, w1, w2, l, first_tile):
    B, S, _ = yb.shape
    nt = S // TM - first_tile
    tok = lambda w: pl.BlockSpec((1, TM, w), lambda j, b: (b, j + first_tile, 0))
    gate = lambda n: pl.BlockSpec((1, TM, D), lambda j, b: (b, j + first_tile, PM_GATE0 * BW // D + n))
    layer = lambda shape: pl.BlockSpec((None,) + shape, lambda j, b: (l,) + (0,) * len(shape),
                                       pipeline_mode=pl.Buffered(1))
    return pl.pallas_call(
        functools.partial(_merge_ffn_kernel, first_tile=first_tile),
        grid=(nt, B),
        in_specs=[
            *_stream_specs(lat_first, first_tile),
            pl.BlockSpec((1, 1, 6 * D), lambda j, b: (_mod_row(b, j + first_tile), 0, 0)),
            pl.BlockSpec((TM, BW), lambda j, b: (j + first_tile, b)),
            tok(BW),
            pl.BlockSpec((1, TM, BW), lambda j, b: (b, j, 0)),
            gate(0), gate(1), gate(2),
            layer((3, BW, D)), layer((D, D)), pl.BlockSpec((1, D), lambda j, b: (0, 0)),
            layer((D, D_FF)), layer((D_FF, D)),
        ],
        out_specs=pl.BlockSpec((1, TM, D), lambda j, b: (b, j, 0)),
        out_shape=jax.ShapeDtypeStruct((B, nt * TM, D), F32),
        compiler_params=_cp(("arbitrary", "arbitrary")),
        name="merge_ffn",
    )(ctx_arr, lat_arr, mod_l, ya, yb, yc, pm, pm, pm, w_branch, w_out, norm_g, w1, w2)


def kernel(x, c, ctx, c_ctx, ada_w, ada_b, norm1_g, norm2_g, w_in, conv_w, conv_b, lru_wa, lru_ba, lru_wx, lru_bx,
           lru_lambda, hg_lb_logits, hg_norm_g, na_q_norm_g, na_k_norm_g, na_rpb, w_branch, w_out, ffn_w1, ffn_w2):
    B, T, _ = x.shape
    S = CTX + T
    cond = jnp.zeros((16, D), F32).at[:B].set(c).at[8].set(c_ctx)
    mod = _modulation(cond, ada_w, ada_b)
    w_in_b, w_out_b, ffn_w1_b, ffn_w2_b = _to_bf16(w_in), _to_bf16(w_out), _to_bf16(ffn_w1), _to_bf16(ffn_w2)
    w_branch_b = _to_bf16(w_branch.reshape(DEPTH, 3 * BW, D)).reshape(w_branch.shape)
    stream = (ctx, x, 1)
    for l in range(DEPTH):
        last = l == DEPTH - 1
        mod_l = mod[l].reshape(16, 1, 6 * D)
        pm, pa, qr, qp, kr = _projection(*stream, mod_l, norm1_g[l][None], w_in_b, l,
                                         jnp.tile(na_q_norm_g[l], NA_H)[None], jnp.tile(na_k_norm_g[l], NA_H)[None])

        ya = _rglru(pa, conv_w[l], conv_b[l][None],
                    _block_diag(lru_wa[l]).astype(BF16), _block_diag(lru_wx[l]).astype(BF16),
                    lru_ba[l][:, None], lru_bx[l][:, None], lru_lambda[l][:, None])
        yb = _hgrn2(pm, hg_lb_logits, jnp.tile(hg_norm_g[l], HG_H)[None], l)
        yc = _attention(qr, qp, kr, pm, _na_bias_tiles(na_rpb[l]), with_ctx=not last)

        h = _merge_ffn(*stream, mod_l, ya.reshape(S, B * BW), yb, yc, pm, w_branch_b, w_out_b, norm2_g[l][None],
                       ffn_w1_b, ffn_w2_b, l, 1 if last else 0)
        stream = (h, h, 0)
    return h
```

```python
import functools

import numpy as np
import jax
import jax.numpy as jnp
from jax import lax
from jax.experimental import pallas as pl
from jax.experimental.pallas import tpu as pltpu

F32 = jnp.float32
BF16 = jnp.bfloat16

D = 1024
DEPTH = 2
GRID_W = 64
CTX = 256
BW = 512
LRU_NB = 8
LRU_BS = BW // LRU_NB
LRU_C = 8.0
HG_H = 4
HG_DK = 128
HG_C = 64
NA_H = 8
NA_DH = 64
WIN_R = 8
WIN_C = 16
ROPE_BASE = 10000.0
MASK_VALUE = -1e30
D_FF = 4 * D
EPS = 1e-6
TM = 256
NA_RB = 4
NA_UR = 12
VMEM_LIMIT = 56 * 1024 * 1024
LOG2E = 1.4426950408889634


def _cp(sem):
    return pltpu.CompilerParams(dimension_semantics=sem, vmem_limit_bytes=VMEM_LIMIT)


def _mod_row(b, j):
    return jnp.where(j == 0, 8, b)


def _mod_kernel(c_ref, w_ref, b_ref, o_ref):
    c = c_ref[...]
    s = c * jax.nn.sigmoid(c)
    o_ref[0] = jnp.dot(s, w_ref[0], preferred_element_type=F32) + b_ref[0]


def _modulation(cond, ada_w, ada_b):
    tn = 1536
    return pl.pallas_call(
        _mod_kernel,
        grid=(DEPTH, 6 * D // tn),
        in_specs=[
            pl.BlockSpec((16, D), lambda l, n: (0, 0)),
            pl.BlockSpec((1, D, tn), lambda l, n: (l, 0, n)),
            pl.BlockSpec((1, 1, tn), lambda l, n: (l, 0, n)),
        ],
        out_specs=pl.BlockSpec((1, 16, tn), lambda l, n: (l, 0, n)),
        out_shape=jax.ShapeDtypeStruct((DEPTH, 16, 6 * D), F32),
        compiler_params=_cp(("arbitrary", "arbitrary")),
        name="modulation",
    )(cond, ada_w, ada_b.reshape(DEPTH, 1, 6 * D))


def _seg_mean_sq(x, seg):
    s = jnp.dot((x * x).astype(BF16), seg, preferred_element_type=F32)
    return s * (1.0 / NA_DH)


def _swap16(y):
    n = y.shape[-1]
    lane = lax.broadcasted_iota(jnp.int32, y.shape, 1)
    return jnp.where(lane % 32 < 16, pltpu.roll(y, n - 16, 1), pltpu.roll(y, 16, 1))


def _rope_tables(S):
    half = NA_DH // 2
    nf = half // 2
    t = np.arange(S - CTX)
    inv_freq = (ROPE_BASE ** (-np.arange(nf, dtype=np.float32) / nf)).astype(np.float32)
    ang_r = (t // GRID_W).astype(np.float32)[:, None] * inv_freq
    ang_c = (t % GRID_W).astype(np.float32)[:, None] * inv_freq
    ang = np.concatenate([ang_r, ang_r, ang_c, ang_c], axis=-1)
    sign = np.concatenate([-np.ones(nf), np.ones(nf), -np.ones(nf), np.ones(nf)]).astype(np.float32)
    cos = np.concatenate([np.ones((CTX, NA_DH), np.float32), np.cos(ang)], axis=0)
    sin = np.concatenate([np.zeros((CTX, NA_DH), np.float32), np.sin(ang) * sign], axis=0)
    return np.tile(cos, (1, NA_H)).astype(np.float32), np.tile(sin, (1, NA_H)).astype(np.float32)


NA_Q_SCALE = NA_DH ** -0.5 * LOG2E
PM_V = 5
PM_GATE0 = 6
PM_COLS = 12 * BW


def _proj_kernel(c_ref, x_ref, mod_ref, g_ref, w_ref, qg_ref, kg_ref, seg_ref, cos_ref, sin_ref,
                 p_ref, pa_ref, qr_ref, qp_ref, kr_ref):
    j = pl.program_id(0)
    b = pl.program_id(1)
    x = jnp.where(j == 0, c_ref[0], x_ref[0])
    m = mod_ref[0]
    y = x * lax.rsqrt(jnp.mean(x * x, axis=-1, keepdims=True) + EPS) * g_ref[...]
    u = (y * (1.0 + m[:, D:2 * D]) + m[:, 0:D]).astype(BF16)
    dot = lambda lo, hi: jnp.dot(u, w_ref[:, lo * BW:hi * BW], preferred_element_type=F32)
    seg = seg_ref[...]
    cos = cos_ref[...]
    sin = sin_ref[...]
    q = dot(7, 8)
    qn = q * lax.rsqrt(_seg_mean_sq(q, seg) + EPS) * qg_ref[...]
    qp_ref[0] = (qn * NA_Q_SCALE).astype(BF16)
    qr_ref[0] = ((qn * cos + _swap16(qn) * sin) * NA_Q_SCALE).astype(BF16)
    k = dot(8, 9)
    kn = k * lax.rsqrt(_seg_mean_sq(k, seg) + EPS) * kg_ref[...]
    kr_ref[0] = (kn * cos + _swap16(kn) * sin).astype(BF16)
    pa_ref[:, pl.ds(b, 1), :] = dot(0, 2)[:, None, :]
    p_ref[0, :, :5 * BW] = dot(2, 7).astype(p_ref.dtype)
    p_ref[0, :, 5 * BW:] = dot(9, 16).astype(p_ref.dtype)


def _stream_specs(lat_first, first_tile=0):
    return (pl.BlockSpec((1, TM, D), lambda j, b: (jnp.where(j + first_tile == 0, b, 0), 0, 0)),
            pl.BlockSpec((1, TM, D), lambda j, b: (b, jnp.maximum(j + first_tile - lat_first, 0), 0)))


def _projection(ctx_arr, lat_arr, lat_first, mod_l, g, w, l, q_g, k_g):
    B = lat_arr.shape[0]
    S = lat_arr.shape[1] + lat_first * TM
    nt = S // TM
    cos, sin = _rope_tables(S)
    seg = np.kron(np.eye(NA_H, dtype=np.float32), np.ones((NA_DH, NA_DH), np.float32))
    const = lambda shape: pl.BlockSpec(shape, lambda j, b: (0,) * len(shape))
    tok = lambda w_: pl.BlockSpec((1, TM, w_), lambda j, b: (b, j, 0))
    return pl.pallas_call(
        _proj_kernel,
        grid=(nt, B),
        in_specs=[
            *_stream_specs(lat_first),
            pl.BlockSpec((1, 1, 6 * D), lambda j, b: (_mod_row(b, j), 0, 0)),
            const((1, D)),
            pl.BlockSpec((None,) + w.shape[1:], lambda j, b: (l, 0, 0), pipeline_mode=pl.Buffered(1)),
            const((1, BW)), const((1, BW)), const((BW, BW)),
            pl.BlockSpec((TM, BW), lambda j, b: (j, 0)), pl.BlockSpec((TM, BW), lambda j, b: (j, 0)),
        ],
        out_specs=[tok(PM_COLS), pl.BlockSpec((TM, B, 2 * BW), lambda j, b: (j, 0, 0)), tok(BW), tok(BW), tok(BW)],
        out_shape=[
            jax.ShapeDtypeStruct((B, S, PM_COLS), BF16),
            jax.ShapeDtypeStruct((S, B, 2 * BW), F32),
            jax.ShapeDtypeStruct((B, S, BW), BF16),
            jax.ShapeDtypeStruct((B, S, BW), BF16),
            jax.ShapeDtypeStruct((B, S, BW), BF16),
        ],
        compiler_params=_cp(("arbitrary", "arbitrary")),
        name="projection",
    )(ctx_arr, lat_arr, mod_l, g, w, q_g, k_g, jnp.asarray(seg, BF16), jnp.asarray(cos), jnp.asarray(sin))


LRU_CW = 256
LRU_TM = 128


def _seq_tile(p, j, nt, nct):
    return jnp.where(p == 0, j, jnp.where(j < nct, nct - 1 - j, nt - 1 - (j - nct)))


def _softplus(x):
    return jnp.maximum(x, 0.0) + jnp.log1p(jnp.exp(-jnp.abs(x)))


def _gelu_tanh(x):
    return 0.5 * x * (1.0 + jnp.tanh(np.sqrt(2.0 / np.pi) * (x + 0.044715 * (x * x * x))))


def _lru_kernel(x_ref, gate_ref, prev_ref, next_ref, cw_ref, cb_ref, wa_ref, wx_ref, ba_ref, bx_ref, lam_ref,
                y_ref, hf_ref, a_ref, b_ref, carry_ref, *, nt, nct):
    TM = LRU_TM
    p = pl.program_id(1)
    j = pl.program_id(2)
    tile = _seq_tile(p, j, nt, nct)
    nb = x_ref.shape[1]

    @pl.when(j == 0)
    def _():
        carry_ref[...] = jnp.zeros_like(carry_ref)

    x = x_ref[...]
    has_prev = jnp.logical_and(tile != 0, tile != nct)
    has_next = jnp.logical_and(tile != nct - 1, tile != nt - 1)
    prev = jnp.where(has_prev, prev_ref[...], 0.0)
    nxt = jnp.where(has_next, next_ref[...], 0.0)
    xe = jnp.concatenate([prev, x, nxt], axis=0)
    cw = cw_ref[...]
    xc = cb_ref[...][None] + xe[0:TM] * cw[0:1][None] + xe[1:TM + 1] * cw[1:2][None] \
        + xe[2:TM + 2] * cw[2:3][None] + xe[3:TM + 3] * cw[3:4][None]
    xc2 = xc.reshape(TM * nb, LRU_CW)
    xb = xc2.astype(BF16)
    r = jax.nn.sigmoid(jnp.dot(xb, wa_ref[0, 0], preferred_element_type=F32) + ba_ref[0])
    i = jax.nn.sigmoid(jnp.dot(xb, wx_ref[0, 0], preferred_element_type=F32) + bx_ref[0])
    a = jnp.exp2(r * ((-LRU_C * LOG2E) * _softplus(-lam_ref[0])))
    bc = jnp.sqrt(1.0 - a * a) * (i * xc2)
    a_ref[...] = a.reshape(TM, nb, LRU_CW)
    b_ref[...] = bc.reshape(TM, nb, LRU_CW)

    def step(t, h):
        h = a_ref[t] * h + b_ref[t]
        b_ref[t] = h
        return h

    @pl.when(p == 0)
    def _():
        carry_ref[...] = lax.fori_loop(0, TM, step, carry_ref[...], unroll=8)
        hf_ref[pl.ds(pl.multiple_of(tile * TM, TM), TM)] = b_ref[...]

    @pl.when(p == 1)
    def _():
        carry_ref[...] = lax.fori_loop(0, TM, lambda s, h: step(TM - 1 - s, h), carry_ref[...], unroll=8)
        hf = hf_ref[pl.ds(pl.multiple_of(tile * TM, TM), TM)]
        y_ref[...] = (hf + b_ref[...]) * _gelu_tanh(gate_ref[...])


def _rglru(pa, conv_w, conv_b, wa_bd, wx_bd, ba, bx, lam):
    S, B, _ = pa.shape
    TM = LRU_TM
    nt = S // TM
    nct = CTX // TM
    ncw = BW // LRU_CW
    tile = functools.partial(_seq_tile, nt=nt, nct=nct)
    vec = lambda: pl.BlockSpec((1, 1, LRU_CW), lambda c, p, j: (p, 0, c))
    return pl.pallas_call(
        functools.partial(_lru_kernel, nt=nt, nct=nct),
        grid=(ncw, 2, nt),
        in_specs=[
            pl.BlockSpec((TM, B, LRU_CW), lambda c, p, j: (tile(p, j), 0, c)),
            pl.BlockSpec((TM, B, LRU_CW), lambda c, p, j: (tile(p, j), 0, ncw + c)),
            pl.BlockSpec((2, B, LRU_CW), lambda c, p, j: (jnp.maximum(tile(p, j) * (TM // 2) - 1, 0), 0, c)),
            pl.BlockSpec((1, B, LRU_CW), lambda c, p, j: (jnp.minimum((tile(p, j) + 1) * TM, S - 1), 0, c)),
            pl.BlockSpec((4, LRU_CW), lambda c, p, j: (0, c)),
            pl.BlockSpec((1, LRU_CW), lambda c, p, j: (0, c)),
            pl.BlockSpec((1, 1, LRU_CW, LRU_CW), lambda c, p, j: (p, c, 0, 0)),
            pl.BlockSpec((1, 1, LRU_CW, LRU_CW), lambda c, p, j: (p, c, 0, 0)),
            vec(), vec(), vec(),
        ],
        out_specs=pl.BlockSpec((TM, B, LRU_CW), lambda c, p, j: (jnp.where(p == 0, nct - 1, tile(p, j)), 0, c)),
        out_shape=jax.ShapeDtypeStruct((S, B, BW), F32),
        scratch_shapes=[
            pltpu.VMEM((S, B, LRU_CW), F32),
            pltpu.VMEM((TM, B, LRU_CW), F32),
            pltpu.VMEM((TM, B, LRU_CW), F32),
            pltpu.VMEM((B, LRU_CW), F32),
        ],
        compiler_params=_cp(("arbitrary", "arbitrary", "arbitrary")),
        name="rglru",
    )(pa, pa, pa, pa, conv_w, conv_b, wa_bd, wx_bd, ba, bx, lam)


def _block_diag(w):
    per = LRU_CW // LRU_BS
    w = w.reshape(2, BW // LRU_CW, per, LRU_BS, LRU_BS)
    eye = jnp.eye(per, dtype=w.dtype)
    out = jnp.einsum('dcnio,nm->dcnimo', w, eye)
    return out.reshape(2, BW // LRU_CW, LRU_CW, LRU_CW)


HG_NLEV = 6
HG_FINE = 3
HG_NB = 4
HG_ROWS = (HG_NLEV + 2) * HG_C


def _logistic(x):
    return 0.5 * jnp.tanh(0.5 * x) + 0.5


def _hgrn_constants():
    C = HG_C
    W = np.zeros((2, HG_NLEV + 2, C, C), np.float32)
    M = np.zeros((2, HG_NLEV + 1, C, C), np.float32)
    for lv in range(HG_NLEV):
        m = C >> (lv + 1)
        for t in range(C):
            blk = t // m
            if blk % 2 == 1:
                p = blk * m
                W[0, lv, t, p:t + 1] = 1.0
                W[1, lv, t, p:t] = 1.0
            else:
                p = (blk + 1) * m
                W[0, lv, t, t + 1:p] = 1.0
                W[1, lv, t, t:p] = 1.0
        for t in range(C):
            for s in range(C):
                if (t // m) % 2 == 1 and s // m == t // m - 1:
                    M[0, lv, t, s] = 1.0
                    M[1, lv, s, t] = 1.0
    for t in range(C):
        W[0, HG_NLEV, t, :t + 1] = 1.0
        W[1, HG_NLEV, t, t:] = 1.0
        W[0, HG_NLEV + 1, t, t + 1:] = 1.0
        W[1, HG_NLEV + 1, t, :t] = 1.0
        M[:, HG_NLEV, t, t] = 1.0
    W = np.concatenate([W[:, HG_NLEV:HG_NLEV + 1], W[:, HG_NLEV - HG_FINE:HG_NLEV]], axis=1)
    W = W.reshape(2, (1 + HG_FINE) * C, C)
    lev = np.full((2, C, C), -1, np.int32)
    for lv in range(HG_NLEV + 1):
        lev[M[:, lv] > 0.5] = lv
    return np.concatenate([W, W], axis=-1), lev


def _hgrn_kernel(q_ref, f_ref, v_ref, og_ref, lbl_ref, ng_ref, w_ref, m_ref, y_ref, of_ref, st_ref, *, nt, layer):
    p = pl.program_id(1)
    j = pl.program_id(2)
    tile = _seq_tile(p, j, nt, CTX // TM)
    C = HG_C
    nchunk = TM // C

    @pl.when(j == 0)
    def _():
        st_ref[...] = jnp.zeros_like(st_ref)

    logits = lbl_ref[0]
    e = jnp.exp(logits - jnp.max(logits, axis=0, keepdims=True))
    soft = e / jnp.sum(e, axis=0, keepdims=True)
    lb = jnp.sum(soft[0:layer + 1], axis=0, keepdims=True) - soft[0:1]

    wmat = w_ref[0]
    lev = m_ref[0]
    row0 = pl.multiple_of(tile * TM, TM)
    inter = HG_NLEV * C

    def prepare(backward, bb):
        f = lb + (1.0 - lb) * _logistic(f_ref[bb].astype(F32))
        g = jnp.log(f) * LOG2E
        k = (1.0 - f).astype(BF16)
        qr = q_ref[bb].astype(F32)
        q = (qr * _logistic(qr)).astype(BF16)
        g1 = g.astype(BF16)
        g2 = (g - g1.astype(F32)).astype(BF16)
        g_split = jnp.concatenate(
            [jnp.concatenate([g1[c * C:(c + 1) * C], g2[c * C:(c + 1) * C]], axis=0) for c in range(nchunk)], axis=1)
        sums = jnp.dot(wmat, g_split, preferred_element_type=F32)
        run = sums[:C]
        coarse = []
        for lv in range(HG_NLEV - HG_FINE):
            m = C >> (lv + 1)
            blocks = []
            for kb in range(C // (2 * m)):
                edge = (2 * kb + 1) * m - (0 if backward else 1)
                ref = run[edge:edge + 1]
                early, late = run[2 * kb * m:(2 * kb + 1) * m], run[(2 * kb + 1) * m:(2 * kb + 2) * m]
                blocks += [early - ref, ref - late] if backward else [ref - early, late - ref]
            coarse.append(jnp.concatenate(blocks, axis=0))
        to_edge = (run[0:1] if backward else run[C - 1:C]) - run
        ex32 = jnp.exp2(jnp.concatenate(coarse + [sums[C:], run, to_edge], axis=0))
        dec_row = inter if backward else inter + C - 1
        return q, k, v_ref[bb], ex32.astype(BF16), ex32[dec_row:dec_row + 1]

    def direction(backward):
        nbat = q_ref.shape[0]
        prep = [prepare(backward, bb) for bb in range(nbat)]
        intra, q_in, upd = {}, {}, {}
        for c in range(nchunk):
            rows = slice(c * C, (c + 1) * C)
            for h in range(HG_H):
                hs = slice(h * HG_DK, (h + 1) * HG_DK)
                for bb in range(nbat):
                    q, k, v, ex, _ = prep[bb]
                    qh, kh, vh = q[rows, hs], k[rows, hs], v[rows, hs]
                    eh = ex[:, c * BW + h * HG_DK:c * BW + (h + 1) * HG_DK].reshape(HG_NLEV + 2, C, HG_DK)
                    qs = jnp.concatenate([qh[None] * eh[:HG_NLEV], qh[None]], axis=0)
                    ks = jnp.concatenate([kh[None] * eh[:HG_NLEV], kh[None]], axis=0)
                    att_l = jnp.einsum('lqd,lkd->lqk', qs, ks, preferred_element_type=F32)
                    att = jnp.zeros((C, C), F32)
                    for lv in range(HG_NLEV + 1):
                        att = jnp.where(lev == lv, att_l[lv], att)
                    intra[bb, c, h] = jnp.dot(att.astype(BF16), vh, preferred_element_type=F32)
                    q_in[bb, c, h] = qh * eh[HG_NLEV]
                    upd[bb, c, h] = lax.dot_general(vh, kh * eh[HG_NLEV + 1], (((0,), (0,)), ((), ())),
                                                    preferred_element_type=F32)
        st = [[st_ref[bb, h] for h in range(HG_H)] for bb in range(nbat)]
        outs = [[None] * nchunk for _ in range(nbat)]
        for c in (reversed(range(nchunk)) if backward else range(nchunk)):
            row = [[] for _ in range(nbat)]
            for h in range(HG_H):
                for bb in range(nbat):
                    dec = prep[bb][4][:, c * BW + h * HG_DK:c * BW + (h + 1) * HG_DK]
                    row[bb].append(intra[bb, c, h]
                                   + lax.dot_general(q_in[bb, c, h], st[bb][h].astype(BF16), (((1,), (1,)), ((), ())),
                                                     preferred_element_type=F32))
                    st[bb][h] = dec * st[bb][h] + upd[bb, c, h]
            for bb in range(nbat):
                outs[bb][c] = jnp.concatenate(row[bb], axis=-1)
        for bb in range(nbat):
            for h in range(HG_H):
                st_ref[bb, h] = st[bb][h]
        return [jnp.concatenate(o, axis=0) for o in outs]

    @pl.when(p == 0)
    def _():
        for bb, o in enumerate(direction(False)):
            of_ref[bb, pl.ds(row0, TM), :] = o

    @pl.when(p == 1)
    def _():
        for bb, o_b in enumerate(direction(True)):
            o = o_b + of_ref[bb, pl.ds(row0, TM), :]
            ys = []
            for h in range(HG_H):
                oh = o[:, h * HG_DK:(h + 1) * HG_DK]
                ys.append(oh * lax.rsqrt(jnp.mean(oh * oh, axis=-1, keepdims=True) + EPS))
            yn = jnp.concatenate(ys, axis=-1) * ng_ref[...]
            y_ref[bb] = (yn * _logistic(og_ref[bb].astype(F32))).astype(y_ref.dtype)


def _hgrn2(pm, lb_logits, norm_g, layer):
    B, S, _ = pm.shape
    nt = S // TM
    tile = functools.partial(_seq_tile, nt=nt, nct=CTX // TM)
    w_np, m_np = _hgrn_constants()
    w_c = jnp.asarray(w_np, BF16)
    m_c = jnp.asarray(m_np, jnp.int32)
    nb = min(HG_NB, B)
    col = lambda cb: pl.BlockSpec((nb, TM, BW), lambda b, p, j: (b, tile(p, j), cb))
    return pl.pallas_call(
        functools.partial(_hgrn_kernel, nt=nt, layer=layer),
        grid=(B // nb, 2, nt),
        in_specs=[
            col(0),
            pl.BlockSpec((nb, TM, BW), lambda b, p, j: (b, tile(p, j), 1 + p)),
            col(3), col(4),
            pl.BlockSpec((1, DEPTH, BW), lambda b, p, j: (p, 0, 0)),
            pl.BlockSpec((1, BW), lambda b, p, j: (0, 0)),
            pl.BlockSpec((1, (1 + HG_FINE) * HG_C, 2 * HG_C), lambda b, p, j: (p, 0, 0)),
            pl.BlockSpec((1, HG_C, HG_C), lambda b, p, j: (p, 0, 0)),
        ],
        out_specs=pl.BlockSpec((nb, TM, BW), lambda b, p, j: (b, jnp.where(p == 0, CTX // TM - 1, tile(p, j)), 0)),
        out_shape=jax.ShapeDtypeStruct((B, S, BW), BF16),
        scratch_shapes=[
            pltpu.VMEM((nb, S, BW), F32),
            pltpu.VMEM((nb, HG_H, HG_DK, HG_DK), F32),
        ],
        compiler_params=_cp(("arbitrary", "arbitrary", "arbitrary")),
        name="hgrn2",
    )(pm, pm, pm, pm, lb_logits, norm_g, w_c, m_c)


NA_GH = 4
NA_GW = NA_GH * NA_DH


def _head_mask(h, n_rows):
    lane = lax.broadcasted_iota(jnp.int32, (n_rows, NA_GW), 1)
    return lane // NA_DH == h


def _softmax_pv(s_parts, v_parts):
    out = jnp.zeros((TM, NA_GW), F32)
    for h in range(NA_GH):
        parts = [s[h * TM:(h + 1) * TM] for s in s_parts]
        m = parts[0].max(axis=-1, keepdims=True)
        for s in parts[1:]:
            m = jnp.maximum(m, s.max(axis=-1, keepdims=True))
        ps = [jnp.exp2(s - m) for s in parts]
        den = ps[0].sum(axis=-1, keepdims=True)
        for pp in ps[1:]:
            den = den + pp.sum(axis=-1, keepdims=True)
        acc = jnp.zeros((TM, NA_GW), F32)
        for pp, vv in zip(ps, v_parts):
            acc = acc + jnp.dot(pp.astype(BF16), vv, preferred_element_type=F32)
        out = out + jnp.where(_head_mask(h, TM), acc * (1.0 / den), 0.0)
    return out.astype(BF16)


def _na_kernel(qr_ref, qp_ref, k0_ref, k1_ref, k2_ref, v0_ref, v1_ref, v2_ref, kc_ref, vc_ref, bias_ref, y_ref,
               *, with_ctx):
    step = pl.program_id(0)
    masks_q = [_head_mask(h, TM) for h in range(NA_GH)]
    rows = 2048 // GRID_W

    def stack_heads(q):
        return jnp.concatenate([jnp.where(masks_q[h], q, jnp.zeros_like(q)) for h in range(NA_GH)], axis=0)

    nt_dims = (((1,), (1,)), ((), ()))

    def bias_tile_index(iq, key_row):
        lat = step - 1 if with_ctx else step
        r = lat * NA_RB + iq
        u0 = _na_key_base(lat) * NA_RB
        d0 = jnp.clip(r - WIN_R // 2, 0, rows - WIN_R) - r + (WIN_R - 1)
        d = u0 + key_row - r + (WIN_R - 1)
        both = jnp.logical_and(d >= d0, d + 1 <= d0 + WIN_R - 1)
        variant = jnp.where(both, 0, jnp.where(d == d0 + WIN_R - 1, 1, jnp.where(d == d0 - 1, 2, 3)))
        return variant, jnp.clip(d + 1, 0, 2 * WIN_R - 1)

    def part_bias(g, i):
        tiles = {}
        for iq in range(NA_RB):
            for jp in range(TM // (2 * GRID_W)):
                tiles[iq, jp] = bias_tile_index(iq, i * NA_RB + 2 * jp)
        blocks = []
        for h in range(NA_GH):
            for iq in range(NA_RB):
                blocks.append(jnp.concatenate(
                    [bias_ref[tiles[iq, jp][0], g * NA_GH + h, tiles[iq, jp][1]]
                     for jp in range(TM // (2 * GRID_W))], axis=1))
        return jnp.concatenate(blocks, axis=0)

    def latent():
        for g in range(NA_H // NA_GH):
            gs = slice(g * NA_GW, (g + 1) * NA_GW)
            q_rot = stack_heads(qr_ref[0, :, gs])
            s_parts, v_parts = [], []
            for i, (k_ref, v_ref) in enumerate(((k0_ref, v0_ref), (k1_ref, v1_ref), (k2_ref, v2_ref))):
                s = lax.dot_general(q_rot, k_ref[0, :, gs], nt_dims, preferred_element_type=F32)
                s_parts.append(s + part_bias(g, i))
                v_parts.append(v_ref[0, :, gs])
            s_parts.append(lax.dot_general(stack_heads(qp_ref[0, :, gs]), kc_ref[0, :, gs], nt_dims,
                                           preferred_element_type=F32))
            v_parts.append(vc_ref[0, :, gs])
            y_ref[0, :, gs] = _softmax_pv(s_parts, v_parts)

    def context():
        for g in range(NA_H // NA_GH):
            gs = slice(g * NA_GW, (g + 1) * NA_GW)
            s_ctx = lax.dot_general(stack_heads(qp_ref[0, :, gs]), kc_ref[0, :, gs], nt_dims,
                                    preferred_element_type=F32)
            y_ref[0, :, gs] = _softmax_pv([s_ctx], [vc_ref[0, :, gs]])

    if with_ctx:
        pl.when(step == 0)(context)
        pl.when(step > 0)(latent)
    else:
        latent()


def _na_bias_tiles(rpb):
    col = np.arange(GRID_W)
    c_start = np.clip(col - WIN_C // 2, 0, GRID_W - WIN_C)
    col_ok = (col[None, :] >= c_start[:, None]) & (col[None, :] < c_start[:, None] + WIN_C)
    d_col = np.clip(col[None, :] - col[:, None], -(WIN_C - 1), WIN_C - 1) + (WIN_C - 1)
    nd = 2 * WIN_R
    hot_c = np.eye(2 * WIN_C - 1, dtype=np.float32)[d_col]
    row_of = np.arange(nd)[:, None] - 1 + np.arange(2)[None, :]
    row_ok = (row_of >= 0) & (row_of < 2 * WIN_R - 1)
    hot_r = np.eye(2 * WIN_R - 1, dtype=np.float32)[np.clip(row_of, 0, 2 * WIN_R - 2)] * row_ok[..., None]
    pairs = jnp.einsum('hrc,djr,qkc->hdqjk', rpb.astype(F32), jnp.asarray(hot_r), jnp.asarray(hot_c),
                       precision=lax.Precision.HIGHEST) * LOG2E
    keep_half = np.array([[1, 1], [1, 0], [0, 1], [0, 0]], bool)
    keep = keep_half[:, None, None, :, None] & row_ok[None, :, None, :, None] & col_ok[None, None, :, None, :]
    tiles = jnp.where(jnp.asarray(keep)[:, None], pairs[None], MASK_VALUE)
    return tiles.reshape(4, NA_H, nd, GRID_W, 2 * GRID_W)


def _na_key_base(step_latent):
    return jnp.clip(step_latent - 1, 0, 2048 // TM - NA_UR * GRID_W // TM)


def _attention(qr, qp, kr, pm, bias, with_ctx):
    B, S, _ = qr.shape
    nlat = (S - CTX) // TM
    off = 0 if with_ctx else 1

    def qtile(t, b):
        return (b, t + off, 0)

    def ktile(i, cb=0):
        def f(t, b):
            return (b, 1 + _na_key_base(t + off - 1) + i, cb)
        return f

    blk = lambda f: pl.BlockSpec((1, TM, BW), f)
    return pl.pallas_call(
        functools.partial(_na_kernel, with_ctx=with_ctx),
        grid=(nlat + (1 if with_ctx else 0), B),
        in_specs=[blk(qtile), blk(qtile), blk(ktile(0)), blk(ktile(1)), blk(ktile(2)),
                  blk(ktile(0, PM_V)), blk(ktile(1, PM_V)), blk(ktile(2, PM_V)),
                  blk(lambda t, b: (b, 0, 0)), blk(lambda t, b: (b, 0, PM_V)),
                  pl.BlockSpec(bias.shape, lambda t, b: (0,) * bias.ndim)],
        out_specs=blk(lambda t, b: (b, t, 0)),
        out_shape=jax.ShapeDtypeStruct((B, S - off * TM, BW), BF16),
        compiler_params=_cp(("arbitrary", "arbitrary")),
        name="attention",
    )(qr, qp, kr, kr, kr, pm, pm, pm, kr, pm, bias)


def _merge_ffn_kernel(c_ref, h_ref, mod_ref, ya_ref, yb_ref, yc_ref, g0_ref, g1_ref, g2_ref, wb_ref, wo_ref, ng_ref,
                      w1_ref, w2_ref, o_ref, *, first_tile):
    h_in = h_ref[0]
    if first_tile == 0:
        h_in = jnp.where(pl.program_id(0) == 0, c_ref[0], h_in)
    m = mod_ref[0]
    acc = None
    for n, (y_ref, g_ref) in enumerate(((ya_ref, g0_ref), (yb_ref, g1_ref), (yc_ref, g2_ref))):
        y = y_ref[...].reshape(TM, BW).astype(BF16)
        term = jax.nn.sigmoid(g_ref[0].astype(F32)) * jnp.dot(y, wb_ref[n], preferred_element_type=F32)
        acc = term if acc is None else acc + term
    mixed = jnp.dot(acc.astype(BF16), wo_ref[...], preferred_element_type=F32)
    x = h_in + m[:, 2 * D:3 * D] * mixed
    y = x * lax.rsqrt(jnp.mean(x * x, axis=-1, keepdims=True) + EPS) * ng_ref[...]
    u = (y * (1.0 + m[:, 4 * D:5 * D]) + m[:, 3 * D:4 * D]).astype(BF16)
    a = jnp.maximum(jnp.dot(u, w1_ref[...], preferred_element_type=F32), 0.0)
    out = jnp.dot((a * a).astype(BF16), w2_ref[...], preferred_element_type=F32)
    o_ref[0] = x + m[:, 5 * D:6 * D] * out


def _merge_ffn(ctx_arr, lat_arr, lat_first, mod_l, ya, yb, yc, pm, w_branch, w_out, norm_g, w1, w2, l, first_tile):
    B, S, _ = yb.shape
    nt = S // TM - first_tile
    tok = lambda w: pl.BlockSpec((1, TM, w), lambda j, b: (b, j + first_tile, 0))
    gate = lambda n: pl.BlockSpec((1, TM, D), lambda j, b: (b, j + first_tile, PM_GATE0 * BW // D + n))
    layer = lambda shape: pl.BlockSpec((None,) + shape, lambda j, b: (l,) + (0,) * len(shape),
                                       pipeline_mode=pl.Buffered(1))
    return pl.pallas_call(
        functools.partial(_merge_ffn_kernel, first_tile=first_tile),
        grid=(nt, B),
        in_specs=[
            *_stream_specs(lat_first, first_tile),
            pl.BlockSpec((1, 1, 6 * D), lambda j, b: (_mod_row(b, j + first_tile), 0, 0)),
            pl.BlockSpec((TM, BW), lambda j, b: (j + first_tile, b)),
            tok(BW),
            pl.BlockSpec((1, TM, BW), lambda j, b: (b, j, 0)),
            gate(0), gate(1), gate(2),
            layer((3, BW, D)), layer((D, D)), pl.BlockSpec((1, D), lambda j, b: (0, 0)),
            layer((D, D_FF)), layer((D_FF, D)),
        ],
        out_specs=pl.BlockSpec((1, TM, D), lambda j, b: (b, j, 0)),
        out_shape=jax.ShapeDtypeStruct((B, nt * TM, D), F32),
        compiler_params=_cp(("arbitrary", "arbitrary")),
        name="merge_ffn",
    )(ctx_arr, lat_arr, mod_l, ya, yb, yc, pm, pm, pm, w_branch, w_out, norm_g, w1, w2)


def kernel(x, c, ctx, c_ctx, ada_w, ada_b, norm1_g, norm2_g, w_in, conv_w, conv_b, lru_wa, lru_ba, lru_wx, lru_bx,
           lru_lambda, hg_lb_logits, hg_norm_g, na_q_norm_g, na_k_norm_g, na_rpb, w_branch, w_out, ffn_w1, ffn_w2):
    B, T, _ = x.shape
    S = CTX + T
    cond = jnp.zeros((16, D), F32).at[:B].set(c).at[8].set(c_ctx)
    mod = _modulation(cond, ada_w, ada_b)
    w_in_b, w_branch_b, w_out_b = w_in.astype(BF16), w_branch.astype(BF16), w_out.astype(BF16)
    ffn_w1_b, ffn_w2_b = ffn_w1.astype(BF16), ffn_w2.astype(BF16)
    stream = (ctx, x, 1)
    for l in range(DEPTH):
        last = l == DEPTH - 1
        mod_l = mod[l].reshape(16, 1, 6 * D)
        pm, pa, qr, qp, kr = _projection(*stream, mod_l, norm1_g[l][None], w_in_b, l,
                                         jnp.tile(na_q_norm_g[l], NA_H)[None], jnp.tile(na_k_norm_g[l], NA_H)[None])

        ya = _rglru(pa, conv_w[l], conv_b[l][None],
                    _block_diag(lru_wa[l]).astype(BF16), _block_diag(lru_wx[l]).astype(BF16),
                    lru_ba[l][:, None], lru_bx[l][:, None], lru_lambda[l][:, None])
        yb = _hgrn2(pm, hg_lb_logits, jnp.tile(hg_norm_g[l], HG_H)[None], l)
        yc = _attention(qr, qp, kr, pm, _na_bias_tiles(na_rpb[l]), with_ctx=not last)

        h = _merge_ffn(*stream, mod_l, ya.reshape(S, B * BW), yb, yc, pm, w_branch_b, w_out_b, norm2_g[l][None],
                       ffn_w1_b, ffn_w2_b, l, 1 if last else 0)
        stream = (h, h, 0)
    return h
```
